```python
import math
import jax, jax.numpy as jnp
from jax import lax
import numpy as np


D_MODEL = 2048
BATCH = 2
SEQ = 8192
DEPTH = 1

HEAD_DIM_A = 128
DILATED_GROUPS = ((128, 1), (512, 4), (2048, 16))
HEADS_PER_GROUP_A = 4
N_HEADS_A = HEADS_PER_GROUP_A * len(DILATED_GROUPS)
WIDTH_A = N_HEADS_A * HEAD_DIM_A
OUT_A = HEADS_PER_GROUP_A * HEAD_DIM_A
HEAD_DIM_B = 64
N_Q_HEADS_B = 16
N_KV_HEADS_B = 2
Q_PER_KV_B = N_Q_HEADS_B // N_KV_HEADS_B
WINDOW_B = 128
WIDTH_QB = N_Q_HEADS_B * HEAD_DIM_B
WIDTH_KVB = N_KV_HEADS_B * HEAD_DIM_B
N_MEM = 256
N_HEADS_C = 4
HEAD_DIM_C = 256
WIDTH_C = N_HEADS_C * HEAD_DIM_C
N_BRANCHES = 3
IN_SPLITS = (WIDTH_A, WIDTH_A, WIDTH_A, WIDTH_QB, WIDTH_KVB, WIDTH_KVB, WIDTH_C, N_BRANCHES * D_MODEL)
IN_WIDTH = sum(IN_SPLITS)
ATTN_BLOCK = 128
ROPE_THETA = 10000.0
N_EXPERTS = 256
TOP_K = 8
N_EXPERT_GROUPS = 8
TOPK_EXPERT_GROUPS = 4
D_EXPERT = D_MODEL // 4
ROUTED_SCALE = 2.5
MOE_BLOCK = 128
ALPHA = (2 * DEPTH) ** 0.25
BETA = (8 * DEPTH) ** -0.25
LN_EPS = 1e-5

kernel_name = 'hybrid_dilated_swa_memory_moe_layer'


def layer_norm(x, g, b):
    xf = x.astype(jnp.float32)
    mu = xf.mean(-1, keepdims=True)
    var = jnp.square(xf - mu).mean(-1, keepdims=True)
    return ((xf - mu) * lax.rsqrt(var + LN_EPS) * g.astype(jnp.float32) + b.astype(jnp.float32)).astype(x.dtype)


def rope(x, pos):
    dh = x.shape[-1]
    half = dh // 2
    inv_freq = jnp.power(ROPE_THETA, -jnp.arange(half, dtype=jnp.float32) * 2.0 / dh)
    ang = pos.astype(jnp.float32)[:, None] * inv_freq[None, :]
    cos, sin = jnp.cos(ang), jnp.sin(ang)
    x1 = x[..., :half].astype(jnp.float32)
    x2 = x[..., half:].astype(jnp.float32)
    return jnp.concatenate([x1 * cos - x2 * sin, x2 * cos + x1 * sin], axis=-1).astype(x.dtype)


def banded_attention(q, k, v, n_back, sink=None):
    n, g, length, dh = q.shape
    blk = math.gcd(ATTN_BLOCK, length)
    nb = length // blk
    n_prev = -(-n_back // blk)
    width = (n_prev + 1) * blk
    kp = jnp.pad(k, ((0, 0), (n_prev * blk, 0), (0, 0))).reshape(n, nb + n_prev, blk, dh)
    vp = jnp.pad(v, ((0, 0), (n_prev * blk, 0), (0, 0))).reshape(n, nb + n_prev, blk, dh)
    kb = jnp.concatenate([kp[:, i:i + nb] for i in range(n_prev + 1)], axis=2)
    vb = jnp.concatenate([vp[:, i:i + nb] for i in range(n_prev + 1)], axis=2)
    qb = q.reshape(n, g, nb, blk, dh)
    s = jnp.einsum('ngbqd,nbkd->ngbqk', qb, kb, preferred_element_type=jnp.float32) * (dh ** -0.5)
    qpos = jnp.arange(nb)[:, None] * blk + jnp.arange(blk)[None, :]
    kpos = (jnp.arange(nb)[:, None] - n_prev) * blk + jnp.arange(width)[None, :]
    dist = qpos[:, :, None] - kpos[:, None, :]
    mask = (dist >= 0) & (dist <= n_back) & (kpos[:, None, :] >= 0)
    s = jnp.where(mask, s, -jnp.inf)
    m = s.max(-1)
    if sink is not None:
        sink = sink.astype(jnp.float32)
        m = jnp.maximum(m, sink)
    p = jnp.exp(s - m[..., None])
    denom = p.sum(-1)
    if sink is not None:
        denom = denom + jnp.exp(sink - m)
    out = jnp.einsum('ngbqk,nbkd->ngbqd', p, vb.astype(jnp.float32)) / denom[..., None]
    lse = m + jnp.log(denom)
    return out.reshape(n, g, length, dh), lse.reshape(n, g, length)


def dilated_attention(q, k, v):
    bsz, _, slen, dh = q.shape
    outs, lses = [], []
    for gi, (w, d) in enumerate(DILATED_GROUPS):
        lo, hi = gi * HEADS_PER_GROUP_A, (gi + 1) * HEADS_PER_GROUP_A
        sub = slen // d

        def gather_stride(t):
            return t[:, lo:hi].reshape(bsz, HEADS_PER_GROUP_A, sub, d, dh).transpose(0, 1, 3, 2, 4).reshape(
                bsz * HEADS_PER_GROUP_A * d, sub, dh)

        o, lse = banded_attention(gather_stride(q)[:, None], gather_stride(k), gather_stride(v), w // d)
        outs.append(o.reshape(bsz, HEADS_PER_GROUP_A, d, sub, dh).transpose(0, 1, 3, 2, 4).reshape(
            bsz, HEADS_PER_GROUP_A, slen, dh))
        lses.append(lse.reshape(bsz, HEADS_PER_GROUP_A, d, sub).transpose(0, 1, 3, 2).reshape(
            bsz, HEADS_PER_GROUP_A, slen))
    wts = jax.nn.softmax(jnp.stack(lses), axis=0)
    y = jnp.einsum('gbhs,gbhsd->bshd', wts, jnp.stack(outs))
    return y.reshape(bsz, slen, OUT_A).astype(q.dtype)


def hybrid_mixer(h, mem, w_in, w_mem_kv, sinks, w_branch_a, w_branch_b, w_branch_c, w_out):
    bsz, slen, d_model = h.shape
    pos = jnp.arange(slen)
    offsets = [int(o) for o in np.cumsum(IN_SPLITS)[:-1]]
    qa, ka, va, qb, kb, vb, qc, gate_logits = jnp.split(h @ w_in, offsets, axis=-1)

    def heads_a(t):
        return t.reshape(bsz, slen, N_HEADS_A, HEAD_DIM_A).transpose(0, 2, 1, 3)
    ya = dilated_attention(rope(heads_a(qa), pos), rope(heads_a(ka), pos), heads_a(va))

    qb = rope(qb.reshape(bsz, slen, N_KV_HEADS_B, Q_PER_KV_B, HEAD_DIM_B).transpose(0, 2, 3, 1, 4), pos)
    qb = qb.reshape(bsz * N_KV_HEADS_B, Q_PER_KV_B, slen, HEAD_DIM_B)
    kb = rope(kb.reshape(bsz, slen, N_KV_HEADS_B, HEAD_DIM_B).transpose(0, 2, 1, 3), pos)
    kb = kb.reshape(bsz * N_KV_HEADS_B, slen, HEAD_DIM_B)
    vb = vb.reshape(bsz, slen, N_KV_HEADS_B, HEAD_DIM_B).transpose(0, 2, 1, 3).reshape(
        bsz * N_KV_HEADS_B, slen, HEAD_DIM_B)
    sink = jnp.broadcast_to(sinks.reshape(1, N_KV_HEADS_B, Q_PER_KV_B), (bsz, N_KV_HEADS_B, Q_PER_KV_B))
    sink = sink.reshape(bsz * N_KV_HEADS_B, Q_PER_KV_B)[:, :, None, None]
    ob, _ = banded_attention(qb, kb, vb, WINDOW_B - 1, sink)
    yb = ob.reshape(bsz, N_KV_HEADS_B, Q_PER_KV_B, slen, HEAD_DIM_B).transpose(0, 3, 1, 2, 4).reshape(
        bsz, slen, WIDTH_QB).astype(h.dtype)

    mk, mv = jnp.split(mem @ w_mem_kv, 2, axis=-1)
    n_mem = mem.shape[1]
    mk = mk.reshape(bsz, n_mem, N_HEADS_C, HEAD_DIM_C)
    mv = mv.reshape(bsz, n_mem, N_HEADS_C, HEAD_DIM_C)
    qc = qc.reshape(bsz, slen, N_HEADS_C, HEAD_DIM_C)
    sc = jnp.einsum('bshd,bmhd->bhsm', qc, mk, preferred_element_type=jnp.float32) * (HEAD_DIM_C ** -0.5)
    pc = jax.nn.softmax(sc, axis=-1)
    yc = jnp.einsum('bhsm,bmhd->bshd', pc, mv.astype(jnp.float32)).reshape(bsz, slen, WIDTH_C).astype(h.dtype)

    gates = jax.nn.sigmoid(gate_logits.astype(jnp.float32)).reshape(bsz, slen, N_BRANCHES, d_model)
    merged = (gates[:, :, 0] * (ya @ w_branch_a) + gates[:, :, 1] * (yb @ w_branch_b)
              + gates[:, :, 2] * (yc @ w_branch_c))
    return merged.astype(h.dtype) @ w_out


def swiglu(x, w_gate, w_up, w_down):
    return (jax.nn.silu(x @ w_gate) * (x @ w_up)) @ w_down


def moe(h, router_w, router_bias, w_exp_gate, w_exp_up, w_exp_down, w_sh_gate, w_sh_up, w_sh_down):
    bsz, slen, d_model = h.shape
    n_tok = bsz * slen
    xt = h.reshape(n_tok, d_model)
    scores = jax.nn.sigmoid(jnp.matmul(xt, router_w, preferred_element_type=jnp.float32))
    biased = scores + router_bias.astype(jnp.float32)
    per_group = N_EXPERTS // N_EXPERT_GROUPS
    group_score = lax.top_k(biased.reshape(n_tok, N_EXPERT_GROUPS, per_group), 2)[0].sum(-1)
    _, group_idx = lax.top_k(group_score, TOPK_EXPERT_GROUPS)
    group_mask = jax.nn.one_hot(group_idx, N_EXPERT_GROUPS, dtype=jnp.float32).sum(1) > 0
    masked = jnp.where(jnp.repeat(group_mask, per_group, axis=1), biased, -jnp.inf)
    _, top_idx = lax.top_k(masked, TOP_K)
    top_w = jnp.take_along_axis(scores, top_idx, axis=1)
    top_w = top_w / top_w.sum(-1, keepdims=True) * ROUTED_SCALE

    n_assign = n_tok * TOP_K
    e_flat = top_idx.reshape(n_assign)
    tok_flat = jnp.repeat(jnp.arange(n_tok), TOP_K)
    w_flat = top_w.reshape(n_assign)
    order = jnp.argsort(e_flat)
    e_s, t_s, w_s = e_flat[order], tok_flat[order], w_flat[order]
    counts = jnp.bincount(e_flat, length=N_EXPERTS)
    starts = jnp.cumsum(counts) - counts
    padded = (counts + MOE_BLOCK - 1) // MOE_BLOCK * MOE_BLOCK
    pad_end = jnp.cumsum(padded)
    pad_start = pad_end - padded
    dest = pad_start[e_s] + (jnp.arange(n_assign) - starts[e_s])
    n_rows = (n_assign + N_EXPERTS * (MOE_BLOCK - 1) + MOE_BLOCK - 1) // MOE_BLOCK * MOE_BLOCK
    n_blocks = n_rows // MOE_BLOCK
    row_tok = jnp.full((n_rows,), n_tok, jnp.int32).at[dest].set(t_s.astype(jnp.int32))
    row_w = jnp.zeros((n_rows,), jnp.float32).at[dest].set(w_s)
    blk_expert = jnp.minimum(jnp.searchsorted(pad_end, jnp.arange(n_blocks) * MOE_BLOCK, side='right'),
                             N_EXPERTS - 1)
    x_pad = jnp.concatenate([xt, jnp.zeros((1, d_model), xt.dtype)], axis=0)

    def expert_block(args):
        tok, wgt, e = args
        xb = x_pad[tok]
        yb = swiglu(xb, w_exp_gate[e], w_exp_up[e], w_exp_down[e])
        return yb.astype(jnp.float32) * wgt[:, None]

    rows = lax.map(expert_block, (row_tok.reshape(n_blocks, MOE_BLOCK), row_w.reshape(n_blocks, MOE_BLOCK),
                                  blk_expert))
    routed = jax.ops.segment_sum(rows.reshape(n_rows, d_model), row_tok, num_segments=n_tok + 1)[:n_tok]
    shared = swiglu(xt, w_sh_gate, w_sh_up, w_sh_down).astype(jnp.float32)
    return (routed + shared).astype(h.dtype).reshape(bsz, slen, d_model)


def setup_inputs(seed: int = 0) -> dict:
    key = jax.random.key(seed)
    ks = jax.random.split(key, 24)
    L, D, E, F = DEPTH, D_MODEL, N_EXPERTS, D_EXPERT

    def nrm(k, shape, scale):
        return jax.random.normal(k, shape, jnp.float32) * scale

    in_col_scale = jnp.concatenate([jnp.full((n,), s, jnp.float32) for n, s in zip(
        IN_SPLITS, (1.0, 1.0, BETA, 1.0, 1.0, BETA, 1.0, 1.0))])
    mem_col_scale = jnp.concatenate([jnp.ones((WIDTH_C,), jnp.float32), jnp.full((WIDTH_C,), BETA, jnp.float32)])
    return {
        'x': nrm(ks[0], (BATCH, SEQ, D), 1.0),
        'mem': nrm(ks[1], (BATCH, N_MEM, D), 1.0),
        'w_in': nrm(ks[2], (L, D, IN_WIDTH), D ** -0.5) * in_col_scale,
        'w_mem_kv': nrm(ks[3], (L, D, 2 * WIDTH_C), D ** -0.5) * mem_col_scale,
        'attn_sinks': nrm(ks[4], (L, N_Q_HEADS_B), 0.5),
        'w_branch_a': nrm(ks[5], (L, OUT_A, D), BETA * OUT_A ** -0.5),
        'w_branch_b': nrm(ks[6], (L, WIDTH_QB, D), BETA * WIDTH_QB ** -0.5),
        'w_branch_c': nrm(ks[7], (L, WIDTH_C, D), BETA * WIDTH_C ** -0.5),
        'w_out': nrm(ks[8], (L, D, D), BETA * D ** -0.5),
        'ln1_g': 1.0 + nrm(ks[9], (L, D), 0.02),
        'ln1_b': nrm(ks[10], (L, D), 0.02),
        'router_w': nrm(ks[11], (L, D, E), D ** -0.5),
        'router_bias': nrm(ks[12], (L, E), 0.01),
        'w_exp_gate': nrm(ks[13], (L, E, D, F), D ** -0.5),
        'w_exp_up': nrm(ks[14], (L, E, D, F), D ** -0.5),
        'w_exp_down': nrm(ks[15], (L, E, F, D), BETA * F ** -0.5),
        'w_sh_gate': nrm(ks[16], (L, D, F), D ** -0.5),
        'w_sh_up': nrm(ks[17], (L, D, F), D ** -0.5),
        'w_sh_down': nrm(ks[18], (L, F, D), BETA * F ** -0.5),
        'ln2_g': 1.0 + nrm(ks[19], (L, D), 0.02),
        'ln2_b': nrm(ks[20], (L, D), 0.02),
    }


def reference(x, mem, w_in, w_mem_kv, attn_sinks, w_branch_a, w_branch_b, w_branch_c, w_out, ln1_g, ln1_b,
              router_w, router_bias, w_exp_gate, w_exp_up, w_exp_down, w_sh_gate, w_sh_up, w_sh_down,
              ln2_g, ln2_b):
    h = x
    for layer in range(DEPTH):
        mix = hybrid_mixer(h, mem, w_in[layer], w_mem_kv[layer], attn_sinks[layer], w_branch_a[layer],
                           w_branch_b[layer], w_branch_c[layer], w_out[layer])
        h = layer_norm(ALPHA * h + mix, ln1_g[layer], ln1_b[layer])
        ff = moe(h, router_w[layer], router_bias[layer], w_exp_gate[layer], w_exp_up[layer], w_exp_down[layer],
                 w_sh_gate[layer], w_sh_up[layer], w_sh_down[layer])
        h = layer_norm(ALPHA * h + ff, ln2_g[layer], ln2_b[layer])
    return h
```

```python
import functools

import jax
import jax.numpy as jnp
from jax import lax
from jax.experimental import pallas as pl
from jax.experimental.pallas import tpu as pltpu

F32 = jnp.float32
BF16 = jnp.bfloat16

DEPTH = 1
HEAD_DIM_A = 128
DILATIONS_A = (1, 4, 16)
N_BACK_A = 128
HEADS_PER_GROUP_A = 4
GROUP_WIDTH_A = HEADS_PER_GROUP_A * HEAD_DIM_A
WIDTH_A = len(DILATIONS_A) * GROUP_WIDTH_A
HEAD_DIM_B = 64
N_Q_HEADS_B = 16
N_KV_HEADS_B = 2
WIDTH_QB = N_Q_HEADS_B * HEAD_DIM_B
WIDTH_KVB = N_KV_HEADS_B * HEAD_DIM_B
N_BACK_B = 127
N_HEADS_C = 4
HEAD_DIM_C = 256
WIDTH_C = N_HEADS_C * HEAD_DIM_C
ROPE_THETA = 10000.0
TOP_K = 8
N_EXPERT_GROUPS = 8
TOPK_EXPERT_GROUPS = 4
ROUTED_SCALE = 2.5
ALPHA = (2 * DEPTH) ** 0.25
LN_EPS = 1e-5

LANES = 128
ATT_BLK = 128
MOE_ROWS = 128
VMEM_LIMIT = 56 * 1024 * 1024
NEG = -1e30


def _cparams(sem):
    return pltpu.CompilerParams(dimension_semantics=sem, vmem_limit_bytes=VMEM_LIMIT)


def _dot(a, b):
    return jnp.dot(a, b, preferred_element_type=F32)


def _dot_nt(a, b):
    return lax.dot_general(a, b, (((1,), (1,)), ((), ())), preferred_element_type=F32)


def _rope_chunk(a, cos, sin, head_dim):
    if head_dim == 128:
        partner = pltpu.roll(a, 64, 1)
    else:
        lane = lax.broadcasted_iota(jnp.int32, a.shape, 1)
        partner = jnp.where((lane % 64) < 32, pltpu.roll(a, 96, 1), pltpu.roll(a, 32, 1))
    return a * cos + partner * sin


def _proj_kernel(*refs, modes, has_tables):
    if has_tables:
        x_ref, w_ref, cos_ref, sin_ref, o_ref = refs
    else:
        x_ref, w_ref, o_ref = refs
        cos_ref = sin_ref = None
    j = pl.program_id(0)
    acc = _dot(x_ref[...], w_ref[...])

    def epilogue(chunk_modes):
        for c, (kind, scale) in enumerate(chunk_modes):
            a = acc[:, c * LANES:(c + 1) * LANES]
            if kind == "rope128":
                a = _rope_chunk(a, cos_ref[...], sin_ref[...], 128)
            elif kind == "rope64":
                a = _rope_chunk(a, cos_ref[...], sin_ref[...], 64)
            elif kind == "sigmoid":
                a = jax.nn.sigmoid(a)
            if scale != 1.0:
                a = a * scale
            o_ref[:, c * LANES:(c + 1) * LANES] = a.astype(o_ref.dtype)

    if all(m == modes[0] for m in modes):
        epilogue(modes[0])
    else:
        for jj, chunk_modes in enumerate(modes):
            pl.when(j == jj)(functools.partial(epilogue, chunk_modes))


def _proj(x, w, modes, tn, out_dtype, tables=None, seq_len=None, tm=1024):
    m, k = x.shape
    n = w.shape[1]
    tm = min(tm, m)
    assert m % tm == 0 and n % tn == 0 and len(modes) == n // tn
    in_specs = [pl.BlockSpec((tm, k), lambda j, i: (i, 0)),
                pl.BlockSpec((k, tn), lambda j, i: (0, j))]
    args = [x, w]
    if tables is not None:
        per_seq = seq_len // tm
        tab_spec = pl.BlockSpec((tm, LANES), lambda j, i: (i % per_seq, 0))
        in_specs += [tab_spec, tab_spec]
        args += list(tables)
    return pl.pallas_call(
        functools.partial(_proj_kernel, modes=modes, has_tables=tables is not None),
        grid=(n // tn, m // tm),
        in_specs=in_specs,
        out_specs=pl.BlockSpec((tm, tn), lambda j, i: (i, j)),
        out_shape=jax.ShapeDtypeStruct((m, n), out_dtype),
        compiler_params=_cparams(("arbitrary", "arbitrary")),
        name="proj",
    )(*args)


def _rope_tables(seq_len, head_dim):
    half = head_dim // 2
    inv_freq = jnp.power(ROPE_THETA, -jnp.arange(half, dtype=F32) * 2.0 / head_dim)
    ang = jnp.arange(seq_len).astype(F32)[:, None] * inv_freq[None, :]
    cos, sin = jnp.cos(ang), jnp.sin(ang)
    reps = LANES // head_dim
    cos_t = jnp.tile(jnp.concatenate([cos, cos], axis=-1), (1, reps))
    sin_t = jnp.tile(jnp.concatenate([-sin, sin], axis=-1), (1, reps))
    return cos_t, sin_t


def _softmax_rows(s):
    m = jnp.max(s, axis=-1, keepdims=True)
    p = jnp.exp(s - m)
    l = jnp.sum(p, axis=-1, keepdims=True)
    return m, p, l


def _attn_a_kernel(first_ref, q_ref, kc_ref, vc_ref, kp_ref, vp_ref, o_ref, lse_ref, *, tq):
    c = pl.program_id(0)
    not_first = first_ref[c] == 0
    row = lax.broadcasted_iota(jnp.int32, (ATT_BLK, 2 * ATT_BLK), 0)
    col = lax.broadcasted_iota(jnp.int32, (ATT_BLK, 2 * ATT_BLK), 1)
    band = (col >= row) & (col <= row + N_BACK_A)
    band_first = band & ((col >= ATT_BLK) | not_first)
    for h in range(HEADS_PER_GROUP_A):
        hs = slice(h * HEAD_DIM_A, (h + 1) * HEAD_DIM_A)
        for i in range(tq // ATT_BLK):
            rows = slice(i * ATT_BLK, (i + 1) * ATT_BLK)
            q = q_ref[rows, hs]
            if i == 0:
                k = jnp.concatenate([kp_ref[:, hs], kc_ref[rows, hs]], axis=0)
                v = jnp.concatenate([vp_ref[:, hs], vc_ref[rows, hs]], axis=0)
                mask = band_first
            else:
                kv_rows = slice((i - 1) * ATT_BLK, (i + 1) * ATT_BLK)
                k = kc_ref[kv_rows, hs]
                v = vc_ref[kv_rows, hs]
                mask = band
            s = jnp.where(mask, _dot_nt(q, k), NEG)
            m, p, l = _softmax_rows(s)
            o = _dot(p.astype(BF16), v) / l
            o_ref[rows, hs] = o.astype(o_ref.dtype)
            lse_ref[rows, hs] = jnp.broadcast_to(m + jnp.log(l), (ATT_BLK, HEAD_DIM_A))


def _attn_a(qkv, first, tq):
    r = qkv.shape[0]
    gw = GROUP_WIDTH_A
    sub = tq // ATT_BLK
    cur = lambda col: pl.BlockSpec((tq, gw), lambda c, f: (c, col))
    prev = lambda col: pl.BlockSpec((ATT_BLK, gw), lambda c, f: (jnp.maximum(c * sub - 1, 0), col))
    return pl.pallas_call(
        functools.partial(_attn_a_kernel, tq=tq),
        grid_spec=pltpu.PrefetchScalarGridSpec(
            num_scalar_prefetch=1,
            grid=(r // tq,),
            in_specs=[cur(0), cur(1), cur(2), prev(1), prev(2)],
            out_specs=[pl.BlockSpec((tq, gw), lambda c, f: (c, 0)),
                       pl.BlockSpec((tq, gw), lambda c, f: (c, 0))],
        ),
        out_shape=[jax.ShapeDtypeStruct((r, gw), BF16), jax.ShapeDtypeStruct((r, gw), F32)],
        compiler_params=_cparams(("arbitrary",)),
        name="attn_a",
    )(first, qkv, qkv, qkv, qkv, qkv)


def _attn_b_kernel(sink_ref, q_ref, kvc_ref, kvp_ref, o_ref, *, tq, chunks_per_seq):
    c = pl.program_id(0)
    not_first = (c % chunks_per_seq) != 0
    row = lax.broadcasted_iota(jnp.int32, (ATT_BLK, 2 * ATT_BLK), 0)
    col = lax.broadcasted_iota(jnp.int32, (ATT_BLK, 2 * ATT_BLK), 1)
    band = (col >= row + 1) & (col <= row + 1 + N_BACK_B)
    band_first = band & ((col >= ATT_BLK) | not_first)
    lane_kv = lax.broadcasted_iota(jnp.int32, (2 * ATT_BLK, LANES), 1) < HEAD_DIM_B
    lane_o = lax.broadcasted_iota(jnp.int32, (ATT_BLK, LANES), 1) < HEAD_DIM_B
    pairs = (N_Q_HEADS_B // N_KV_HEADS_B) // 2
    for i in range(tq // ATT_BLK):
        rows = slice(i * ATT_BLK, (i + 1) * ATT_BLK)
        if i == 0:
            kv = jnp.concatenate([kvp_ref[...], kvc_ref[rows, :]], axis=0)
            mask = band_first
        else:
            kv = kvc_ref[(i - 1) * ATT_BLK:(i + 1) * ATT_BLK, :]
            mask = band
        for n in range(N_KV_HEADS_B):
            kk = kv[:, n * LANES:(n + 1) * LANES]
            vv = kv[:, (N_KV_HEADS_B + n) * LANES:(N_KV_HEADS_B + n + 1) * LANES]
            zero = jnp.zeros_like(kk)
            k_bd = jnp.concatenate([jnp.where(lane_kv, kk, zero), jnp.where(lane_kv, zero, kk)], axis=0)
            v_bd = jnp.concatenate([jnp.where(lane_kv, vv, zero), jnp.where(lane_kv, zero, vv)], axis=0)
            for jp in range(pairs):
                c0 = n * (WIDTH_QB // N_KV_HEADS_B) + jp * LANES
                q = q_ref[rows, c0:c0 + LANES]
                s2 = _dot_nt(q, k_bd)
                ps, invs = [], []
                for hh in range(2):
                    sink = sink_ref[n * 2 * pairs + jp * 2 + hh]
                    s = jnp.where(mask, s2[:, hh * 2 * ATT_BLK:(hh + 1) * 2 * ATT_BLK], NEG)
                    m = jnp.maximum(jnp.max(s, axis=-1, keepdims=True), sink)
                    p = jnp.exp(s - m)
                    denom = jnp.sum(p, axis=-1, keepdims=True) + jnp.exp(sink - m)
                    ps.append(p.astype(BF16))
                    invs.append(1.0 / denom)
                o = _dot(jnp.concatenate(ps, axis=1), v_bd)
                o = o * jnp.where(lane_o, invs[0], invs[1])
                o_ref[rows, c0:c0 + LANES] = o.astype(o_ref.dtype)


def _attn_b(qkv, sinks, seq_len, tq):
    t = qkv.shape[0]
    sub = tq // ATT_BLK
    kv_col = WIDTH_QB // (4 * LANES)
    return pl.pallas_call(
        functools.partial(_attn_b_kernel, tq=tq, chunks_per_seq=seq_len // tq),
        grid=(t // tq,),
        in_specs=[pl.BlockSpec(memory_space=pltpu.SMEM),
                  pl.BlockSpec((tq, WIDTH_QB), lambda c: (c, 0)),
                  pl.BlockSpec((tq, 4 * LANES), lambda c: (c, kv_col)),
                  pl.BlockSpec((ATT_BLK, 4 * LANES), lambda c: (jnp.maximum(c * sub - 1, 0), kv_col))],
        out_specs=pl.BlockSpec((tq, WIDTH_QB), lambda c: (c, 0)),
        out_shape=jax.ShapeDtypeStruct((t, WIDTH_QB), BF16),
        compiler_params=_cparams(("arbitrary",)),
        name="attn_b",
    )(sinks, qkv, qkv, qkv)


def _attn_c_kernel(q_ref, mkv_ref, o_ref, *, tq, sub):
    for h in range(N_HEADS_C):
        hs = slice(h * HEAD_DIM_C, (h + 1) * HEAD_DIM_C)
        mk = mkv_ref[:, hs]
        mv = mkv_ref[:, WIDTH_C + h * HEAD_DIM_C:WIDTH_C + (h + 1) * HEAD_DIM_C]
        for i in range(tq // sub):
            rows = slice(i * sub, (i + 1) * sub)
            s = _dot_nt(q_ref[rows, hs], mk)
            _, p, l = _softmax_rows(s)
            o_ref[rows, hs] = (_dot(p.astype(BF16), mv) / l).astype(o_ref.dtype)


def _attn_c(qc, mkv, seq_len, n_mem, tq):
    t = qc.shape[0]
    per_seq = seq_len // tq
    return pl.pallas_call(
        functools.partial(_attn_c_kernel, tq=tq, sub=min(tq, 256)),
        grid=(t // tq,),
        in_specs=[pl.BlockSpec((tq, WIDTH_C), lambda c: (c, 0)),
                  pl.BlockSpec((n_mem, 2 * WIDTH_C), lambda c: (c // per_seq, 0))],
        out_specs=pl.BlockSpec((tq, WIDTH_C), lambda c: (c, 0)),
        out_shape=jax.ShapeDtypeStruct((t, WIDTH_C), BF16),
        compiler_params=_cparams(("arbitrary",)),
        name="attn_c",
    )(qc, mkv)


def _layer_norm(z, g, b):
    mu = jnp.mean(z, axis=-1, keepdims=True)
    zc = z - mu
    var = jnp.mean(zc * zc, axis=-1, keepdims=True)
    return zc * lax.rsqrt(var + LN_EPS) * g + b


def _merge_kernel(oa_ref, lse_ref, yb_ref, yc_ref, gate_ref, x_ref, wa_ref, wb_ref, wc_ref, wo_ref,
                  g_ref, b_ref, h_ref, *, d_model):
    lses = [lse_ref[g] for g in range(len(DILATIONS_A))]
    top = functools.reduce(jnp.maximum, lses)
    es = [jnp.exp(l - top) for l in lses]
    tot = functools.reduce(jnp.add, es)
    ya = functools.reduce(jnp.add, [e * oa_ref[g].astype(F32) for g, e in enumerate(es)]) / tot
    gate = lambda g: gate_ref[:, g * d_model:(g + 1) * d_model].astype(F32)
    merged = gate(0) * _dot(ya.astype(BF16), wa_ref[...])
    merged += gate(1) * _dot(yb_ref[...], wb_ref[...])
    merged += gate(2) * _dot(yc_ref[...], wc_ref[...])
    mix = _dot(merged.astype(BF16), wo_ref[...])
    h_ref[...] = _layer_norm(ALPHA * x_ref[...] + mix, g_ref[...], b_ref[...])


def _merge(oa, lse, yb, yc, gates, x2, wa, wb, wc, wo, ln_g, ln_b, tm):
    t, d_model = x2.shape
    row = lambda w: pl.BlockSpec((tm, w), lambda i: (i, 0))
    grp = lambda w: pl.BlockSpec((len(DILATIONS_A), tm, w), lambda i: (0, i, 0))
    const = lambda a: pl.BlockSpec(a.shape, lambda i: (0, 0), pipeline_mode=pl.Buffered(1))
    return pl.pallas_call(
        functools.partial(_merge_kernel, d_model=d_model),
        grid=(t // tm,),
        in_specs=[grp(GROUP_WIDTH_A), grp(GROUP_WIDTH_A), row(WIDTH_QB), row(WIDTH_C),
                  row(3 * d_model), row(d_model), const(wa), const(wb), const(wc), const(wo),
                  const(ln_g), const(ln_b)],
        out_specs=row(d_model),
        out_shape=jax.ShapeDtypeStruct((t, d_model), F32),
        compiler_params=_cparams(("arbitrary",)),
        name="merge",
    )(oa, lse, yb, yc, gates, x2, wa, wb, wc, wo, ln_g, ln_b)


def _router_kernel(h_ref, whi_ref, wlo_ref, bias_ref, idx_ref, wgt_ref, *, n_experts):
    h = h_ref[...]
    h_hi = h.astype(BF16)
    h_lo = (h - h_hi.astype(F32)).astype(BF16)
    logits = _dot_nt(whi_ref[...], h_hi) + _dot_nt(whi_ref[...], h_lo) + _dot_nt(wlo_ref[...], h_hi)
    scores = jax.nn.sigmoid(logits)
    biased = scores + bias_ref[...]
    tm = scores.shape[1]
    per_group = n_experts // N_EXPERT_GROUPS
    neg_inf = -jnp.inf

    b3 = biased.reshape(N_EXPERT_GROUPS, per_group, tm)
    i3 = lax.broadcasted_iota(jnp.int32, b3.shape, 1)
    m1 = jnp.max(b3, axis=1, keepdims=True)
    a1 = jnp.min(jnp.where(b3 == m1, i3, per_group), axis=1, keepdims=True)
    m2 = jnp.max(jnp.where(i3 == a1, neg_inf, b3), axis=1, keepdims=True)
    gscore = (m1 + m2).reshape(N_EXPERT_GROUPS, tm)

    gi = lax.broadcasted_iota(jnp.int32, gscore.shape, 0)
    chosen = jnp.zeros(gscore.shape, jnp.int32)
    for _ in range(TOPK_EXPERT_GROUPS):
        gm = jnp.max(gscore, axis=0, keepdims=True)
        ga = jnp.min(jnp.where(gscore == gm, gi, N_EXPERT_GROUPS), axis=0, keepdims=True)
        hit = gi == ga
        chosen = jnp.where(hit, 1, chosen)
        gscore = jnp.where(hit, neg_inf, gscore)

    cur = jnp.where(chosen.reshape(N_EXPERT_GROUPS, 1, tm) > 0, b3, neg_inf).reshape(n_experts, tm)
    ei = lax.broadcasted_iota(jnp.int32, cur.shape, 0)
    idxs, wgts = [], []
    for _ in range(TOP_K):
        m = jnp.max(cur, axis=0, keepdims=True)
        a = jnp.min(jnp.where(cur == m, ei, n_experts), axis=0, keepdims=True)
        hit = ei == a
        idxs.append(a)
        wgts.append(jnp.sum(jnp.where(hit, scores, 0.0), axis=0, keepdims=True))
        cur = jnp.where(hit, neg_inf, cur)
    wsum = functools.reduce(jnp.add, wgts)
    idx_ref[...] = jnp.concatenate(idxs, axis=0)
    wgt_ref[...] = jnp.concatenate(wgts, axis=0) / wsum * ROUTED_SCALE


def _router(h1, w_hi, w_lo, bias, tm):
    t, d_model = h1.shape
    n_experts = w_hi.shape[0]
    const = lambda a: pl.BlockSpec(a.shape, lambda i: (0, 0))
    return pl.pallas_call(
        functools.partial(_router_kernel, n_experts=n_experts),
        grid=(t // tm,),
        in_specs=[pl.BlockSpec((tm, d_model), lambda i: (i, 0)), const(w_hi), const(w_lo), const(bias)],
        out_specs=[pl.BlockSpec((TOP_K, tm), lambda i: (0, i)),
                   pl.BlockSpec((TOP_K, tm), lambda i: (0, i))],
        out_shape=[jax.ShapeDtypeStruct((TOP_K, t), jnp.int32), jax.ShapeDtypeStruct((TOP_K, t), F32)],
        compiler_params=_cparams(("arbitrary",)),
        name="router",
    )(h1, w_hi, w_lo, bias)


def _expert_kernel(blk0_ref, nblk_ref, slot_ref, h_hbm, wg_ref, wu_ref, wd_ref, out_hbm,
                   wgu_s, wd_s, xbuf, ybuf, gsem, ssem, *, n_tok, d_expert):
    n_real_rows = TOP_K * n_tok
    e = pl.program_id(0)
    nb = nblk_ref[e]
    b0 = blk0_ref[e]

    def gather_copy(gb, r, buf):
        tok = slot_ref[gb, r] & (n_tok - 1)
        return pltpu.make_async_copy(h_hbm.at[pl.ds(tok, 1), :], xbuf.at[buf, pl.ds(r, 1), :], gsem.at[buf])

    def scatter_copy(gb, r, buf):
        return pltpu.make_async_copy(ybuf.at[buf, pl.ds(r, 1), :],
                                     out_hbm.at[pl.ds(slot_ref[gb, r], 1), :], ssem.at[buf])

    def start_gather(gb, buf):
        lax.fori_loop(0, MOE_ROWS, lambda r, _: (gather_copy(gb, r, buf).start(), 0)[1], 0)

    def start_scatter(gb, buf):
        lax.fori_loop(0, MOE_ROWS, lambda r, _: (scatter_copy(gb, r, buf).start(), 0)[1], 0)

    def wait_gather(buf):
        for r in range(MOE_ROWS):
            pltpu.make_async_copy(h_hbm.at[pl.ds(0, 1), :], xbuf.at[buf, pl.ds(r, 1), :], gsem.at[buf]).wait()

    def wait_scatter(buf):
        for r in range(MOE_ROWS):
            pltpu.make_async_copy(ybuf.at[buf, pl.ds(r, 1), :], out_hbm.at[pl.ds(0, 1), :], ssem.at[buf]).wait()

    @pl.when(e == 0)
    def _():
        ybuf[...] = jnp.zeros_like(ybuf)
        for buf in range(2):
            tail = pltpu.make_async_copy(
                ybuf.at[buf], out_hbm.at[pl.ds(n_real_rows + buf * MOE_ROWS, MOE_ROWS), :], ssem.at[buf])
            tail.start()
            tail.wait()

    @pl.when(nb > 0)
    def _():
        start_gather(b0, 0)
        wgu_s[:, :d_expert] = wg_ref[0].astype(BF16)
        wgu_s[:, d_expert:] = wu_ref[0].astype(BF16)
        wd_s[...] = wd_ref[0].astype(BF16)

        def block(b, _):
            buf = b % 2
            gb = b0 + b

            @pl.when(b + 1 < nb)
            def _():
                start_gather(gb + 1, 1 - buf)

            wait_gather(buf)
            x = xbuf[buf].astype(BF16)
            gu = _dot(x, wgu_s[...])
            act = gu[:, :d_expert]
            hid = (act * jax.nn.sigmoid(act) * gu[:, d_expert:]).astype(BF16)
            y = _dot(hid, wd_s[...])

            @pl.when(b >= 2)
            def _():
                wait_scatter(buf)

            ybuf[buf] = y
            start_scatter(gb, buf)
            return 0

        lax.fori_loop(0, nb, block, 0)

        @pl.when(nb >= 2)
        def _():
            wait_scatter(nb % 2)

        wait_scatter((nb - 1) % 2)


def _experts(blk0, nblk, slots, h1, w_gate, w_up, w_down, n_out_rows):
    n_tok, d_model = h1.shape
    n_experts, _, d_expert = w_gate.shape
    assert n_tok & (n_tok - 1) == 0
    return pl.pallas_call(
        functools.partial(_expert_kernel, n_tok=n_tok, d_expert=d_expert),
        grid_spec=pltpu.PrefetchScalarGridSpec(
            num_scalar_prefetch=3,
            grid=(n_experts,),
            in_specs=[pl.BlockSpec(memory_space=pl.ANY),
                      pl.BlockSpec((1, d_model, d_expert), lambda e, *_: (e, 0, 0)),
                      pl.BlockSpec((1, d_model, d_expert), lambda e, *_: (e, 0, 0)),
                      pl.BlockSpec((1, d_expert, d_model), lambda e, *_: (e, 0, 0))],
            out_specs=pl.BlockSpec(memory_space=pl.ANY),
            scratch_shapes=[pltpu.VMEM((d_model, 2 * d_expert), BF16),
                            pltpu.VMEM((d_expert, d_model), BF16),
                            pltpu.VMEM((2, MOE_ROWS, d_model), F32),
                            pltpu.VMEM((2, MOE_ROWS, d_model), F32),
                            pltpu.SemaphoreType.DMA((2,)),
                            pltpu.SemaphoreType.DMA((2,))],
        ),
        out_shape=jax.ShapeDtypeStruct((n_out_rows, d_model), F32),
        compiler_params=_cparams(("arbitrary",)),
        name="experts",
    )(blk0, nblk, slots, h1, w_gate, w_up, w_down)


def _dispatch_tables(top_idx, n_experts):
    k, t = top_idx.shape
    n_assign = k * t
    e_flat = top_idx.reshape(n_assign)
    order = jnp.argsort(e_flat, stable=True).astype(jnp.int32)
    e_sorted = e_flat[order]
    counts = jnp.zeros((n_experts,), jnp.int32).at[e_flat].add(1)
    starts = jnp.cumsum(counts) - counts
    nblk = (counts + MOE_ROWS - 1) // MOE_ROWS
    blk_end = jnp.cumsum(nblk)
    blk0 = blk_end - nblk
    dest = blk0[e_sorted] * MOE_ROWS + (jnp.arange(n_assign, dtype=jnp.int32) - starts[e_sorted])
    n_blocks = n_assign // MOE_ROWS + n_experts
    n_rows = n_blocks * MOE_ROWS
    pad_slot = n_assign + jnp.arange(n_rows, dtype=jnp.int32) % (2 * MOE_ROWS)
    slots = pad_slot.at[dest].set(order).reshape(n_blocks, MOE_ROWS)
    return blk0.astype(jnp.int32), nblk.astype(jnp.int32), slots


def _final_kernel(*refs, d_expert):
    h_ref, wt_ref = refs[0], refs[1]
    routed_refs = refs[2:2 + TOP_K]
    wgu_ref, wd_ref, g_ref, b_ref, o_ref = refs[2 + TOP_K:]
    h = h_ref[...]
    gu = _dot(h.astype(BF16), wgu_ref[...])
    act = gu[:, :d_expert]
    hid = (act * jax.nn.sigmoid(act) * gu[:, d_expert:]).astype(BF16)
    ff = _dot(hid, wd_ref[...])
    wt = wt_ref[...]
    for k in range(TOP_K):
        ff += routed_refs[k][...] * wt[:, k:k + 1]
    o_ref[...] = _layer_norm(ALPHA * h + ff, g_ref[...], b_ref[...])


def _final(h1, wt, routed, w_gu, w_d, ln_g, ln_b, tm):
    t, d_model = h1.shape
    d_expert = w_d.shape[0]
    per_k = t // tm
    row = lambda w: pl.BlockSpec((tm, w), lambda i: (i, 0))
    const = lambda a: pl.BlockSpec(a.shape, lambda i: (0, 0))
    routed_specs = [pl.BlockSpec((tm, d_model), lambda i, k=k: (k * per_k + i, 0)) for k in range(TOP_K)]
    return pl.pallas_call(
        functools.partial(_final_kernel, d_expert=d_expert),
        grid=(t // tm,),
        in_specs=[row(d_model), row(TOP_K)] + routed_specs + [const(w_gu), const(w_d), const(ln_g), const(ln_b)],
        out_specs=row(d_model),
        out_shape=jax.ShapeDtypeStruct((t, d_model), F32),
        compiler_params=_cparams(("arbitrary",)),
        name="final",
    )(h1, wt, *([routed] * TOP_K), w_gu, w_d, ln_g, ln_b)


def _to_subsequences(a, bsz, seq_len, d):
    c = a.shape[1]
    return a.reshape(bsz, seq_len // d, d, c).transpose(0, 2, 1, 3).reshape(bsz * seq_len, c)


def _from_subsequences(a, bsz, seq_len, d):
    c = a.shape[1]
    return a.reshape(bsz, d, seq_len // d, c).transpose(0, 2, 1, 3).reshape(bsz * seq_len, c)


def _layer(h, mem, w_in, w_mem_kv, sinks, w_a, w_b, w_c, w_out, ln1_g, ln1_b, router_w, router_bias,
           w_eg, w_eu, w_ed, w_sg, w_su, w_sd, ln2_g, ln2_b):
    bsz, seq_len, d_model = h.shape
    n_tok = bsz * seq_len
    n_mem = mem.shape[1]
    n_experts = router_w.shape[1]
    x2 = h.reshape(n_tok, d_model)
    xb = x2.astype(BF16)

    o_qa, o_ka, o_va = 0, WIDTH_A, 2 * WIDTH_A
    o_qb = 3 * WIDTH_A
    o_kb = o_qb + WIDTH_QB
    o_vb = o_kb + WIDTH_KVB
    o_qc = o_vb + WIDTH_KVB
    o_gate = o_qc + WIDTH_C
    wb16 = w_in.astype(BF16)
    cols = lambda o, n: wb16[:, o:o + n]
    chunks_a = WIDTH_A // LANES
    scale_a = HEAD_DIM_A ** -0.5
    tabs_a = _rope_tables(seq_len, HEAD_DIM_A)
    qkv_a = _proj(xb, cols(0, 3 * WIDTH_A),
                  ((("rope128", scale_a),) * chunks_a, (("rope128", 1.0),) * chunks_a,
                   (("plain", 1.0),) * chunks_a),
                  WIDTH_A, BF16, tabs_a, seq_len)

    hb = HEAD_DIM_B
    dup = lambda o: [cols(o + n * hb, hb) for n in range(N_KV_HEADS_B) for _ in range(2)]
    w_qkv_b = jnp.concatenate([cols(o_qb, WIDTH_QB)] + dup(o_kb) + dup(o_vb), axis=1)
    q_chunks = WIDTH_QB // LANES
    modes_b = ((("rope64", HEAD_DIM_B ** -0.5),) * q_chunks + (("rope64", 1.0),) * N_KV_HEADS_B
               + (("plain", 1.0),) * N_KV_HEADS_B,)
    qkv_b = _proj(xb, w_qkv_b, modes_b, w_qkv_b.shape[1], BF16, _rope_tables(seq_len, HEAD_DIM_B), seq_len)

    q_c = _proj(xb, cols(o_qc, WIDTH_C), ((("plain", HEAD_DIM_C ** -0.5),) * (WIDTH_C // LANES),),
                WIDTH_C, BF16)
    gate_tn = 1536
    gates = _proj(xb, cols(o_gate, 3 * d_model),
                  ((("sigmoid", 1.0),) * (gate_tn // LANES),) * (3 * d_model // gate_tn), gate_tn, BF16)
    mkv = _proj(mem.reshape(bsz * n_mem, d_model).astype(BF16), w_mem_kv.astype(BF16),
                ((("plain", 1.0),) * (2 * WIDTH_C // LANES),), 2 * WIDTH_C, BF16)

    tq = min(512, seq_len // max(DILATIONS_A))
    parts, firsts = [], []
    for gi, d in enumerate(DILATIONS_A):
        grp = jnp.concatenate([qkv_a[:, o + gi * GROUP_WIDTH_A:o + (gi + 1) * GROUP_WIDTH_A]
                               for o in (o_qa, o_ka, o_va)], axis=1)
        parts.append(_to_subsequences(grp, bsz, seq_len, d))
        sub_chunks = seq_len // d // tq
        firsts.append((jnp.arange(n_tok // tq, dtype=jnp.int32) % sub_chunks == 0).astype(jnp.int32))
    o_sub, lse_sub = _attn_a(jnp.concatenate(parts, axis=0), jnp.concatenate(firsts), tq)
    oa = jnp.stack([_from_subsequences(o_sub[gi * n_tok:(gi + 1) * n_tok], bsz, seq_len, d)
                    for gi, d in enumerate(DILATIONS_A)])
    lse = jnp.stack([_from_subsequences(lse_sub[gi * n_tok:(gi + 1) * n_tok], bsz, seq_len, d)
                     for gi, d in enumerate(DILATIONS_A)])

    yb = _attn_b(qkv_b, sinks.astype(F32), seq_len, tq)
    yc = _attn_c(q_c, mkv, seq_len, n_mem, tq)

    h1 = _merge(oa, lse, yb, yc, gates, x2, w_a.astype(BF16), w_b.astype(BF16), w_c.astype(BF16),
                w_out.astype(BF16), ln1_g.reshape(1, d_model), ln1_b.reshape(1, d_model), tm=256)

    rw_t = router_w.T
    rw_hi = rw_t.astype(BF16)
    rw_lo = (rw_t - rw_hi.astype(F32)).astype(BF16)
    top_idx, top_w = _router(h1, rw_hi, rw_lo, router_bias.reshape(n_experts, 1).astype(F32), tm=256)
    blk0, nblk, slots = _dispatch_tables(top_idx, n_experts)
    routed = _experts(blk0, nblk, slots, h1, w_eg, w_eu, w_ed, TOP_K * n_tok + 2 * MOE_ROWS)
    w_sgu = jnp.concatenate([w_sg, w_su], axis=1).astype(BF16)
    out = _final(h1, top_w.T, routed, w_sgu, w_sd.astype(BF16), ln2_g.reshape(1, d_model),
                 ln2_b.reshape(1, d_model), tm=256)
    return out.reshape(bsz, seq_len, d_model)


def kernel(x, mem, w_in, w_mem_kv, attn_sinks, w_branch_a, w_branch_b, w_branch_c, w_out, ln1_g, ln1_b,
           router_w, router_bias, w_exp_gate, w_exp_up, w_exp_down, w_sh_gate, w_sh_up, w_sh_down,
           ln2_g, ln2_b):
    h = x
    for layer in range(DEPTH):
        h = _layer(h, mem, w_in[layer], w_mem_kv[layer], attn_sinks[layer], w_branch_a[layer],
                   w_branch_b[layer], w_branch_c[layer], w_out[layer], ln1_g[layer], ln1_b[layer],
                   router_w[layer], router_bias[layer], w_exp_gate[layer], w_exp_up[layer],
                   w_exp_down[layer], w_sh_gate[layer], w_sh_up[layer], w_sh_down[layer],
                   ln2_g[layer], ln2_b[layer])
    return h
```

```python
import functools

import jax
import jax.numpy as jnp
from jax import lax
from jax.experimental import pallas as pl
from jax.experimental.pallas import tpu as pltpu

F32 = jnp.float32
BF16 = jnp.bfloat16

DEPTH = 1
HEAD_DIM_A = 128
DILATIONS_A = (1, 4, 16)
N_BACK_A = 128
HEADS_PER_GROUP_A = 4
GROUP_WIDTH_A = HEADS_PER_GROUP_A * HEAD_DIM_A
WIDTH_A = len(DILATIONS_A) * GROUP_WIDTH_A
HEAD_DIM_B = 64
N_Q_HEADS_B = 16
N_KV_HEADS_B = 2
WIDTH_QB = N_Q_HEADS_B * HEAD_DIM_B
WIDTH_KVB = N_KV_HEADS_B * HEAD_DIM_B
N_BACK_B = 127
N_HEADS_C = 4
HEAD_DIM_C = 256
WIDTH_C = N_HEADS_C * HEAD_DIM_C
ROPE_THETA = 10000.0
TOP_K = 8
N_EXPERT_GROUPS = 8
TOPK_EXPERT_GROUPS = 4
ROUTED_SCALE = 2.5
ALPHA = (2 * DEPTH) ** 0.25
LN_EPS = 1e-5

LANES = 128
ATT_BLK = 128
MOE_ROWS = 128
EXPERT_PARTS = 2
ISSUE_GROUP = 16
TAB_RING = 4
VMEM_LIMIT = 56 * 1024 * 1024
NEG = -1e30


def _cparams(sem):
    return pltpu.CompilerParams(dimension_semantics=sem, vmem_limit_bytes=VMEM_LIMIT)


def _dot(a, b):
    return jnp.dot(a, b, preferred_element_type=F32)


def _dot_nt(a, b):
    return lax.dot_general(a, b, (((1,), (1,)), ((), ())), preferred_element_type=F32)


def _rope_chunk(a, cos, sin, head_dim):
    if head_dim == 128:
        partner = pltpu.roll(a, 64, 1)
    else:
        lane = lax.broadcasted_iota(jnp.int32, a.shape, 1)
        partner = jnp.where((lane % 64) < 32, pltpu.roll(a, 96, 1), pltpu.roll(a, 32, 1))
    return a * cos + partner * sin


def _proj_kernel(*refs, modes, has_tables):
    if has_tables:
        x_ref, w_ref, cos_ref, sin_ref, o_ref = refs
    else:
        x_ref, w_ref, o_ref = refs
        cos_ref = sin_ref = None
    j = pl.program_id(0)
    acc = _dot(x_ref[...], w_ref[...])

    def epilogue(chunk_modes):
        for c, (kind, scale) in enumerate(chunk_modes):
            a = acc[:, c * LANES:(c + 1) * LANES]
            if kind == "rope128":
                a = _rope_chunk(a, cos_ref[...], sin_ref[...], 128)
            elif kind == "rope64":
                a = _rope_chunk(a, cos_ref[...], sin_ref[...], 64)
            elif kind == "sigmoid":
                a = jax.nn.sigmoid(a)
            if scale != 1.0:
                a = a * scale
            o_ref[:, c * LANES:(c + 1) * LANES] = a.astype(o_ref.dtype)

    if all(m == modes[0] for m in modes):
        epilogue(modes[0])
    else:
        for jj, chunk_modes in enumerate(modes):
            pl.when(j == jj)(functools.partial(epilogue, chunk_modes))


def _proj(x, w, modes, tn, out_dtype, tables=None, seq_len=None, tm=1024):
    m, k = x.shape
    n = w.shape[1]
    tm = min(tm, m)
    assert m % tm == 0 and n % tn == 0 and len(modes) == n // tn
    in_specs = [pl.BlockSpec((tm, k), lambda j, i: (i, 0)),
                pl.BlockSpec((k, tn), lambda j, i: (0, j))]
    args = [x, w]
    if tables is not None:
        per_seq = seq_len // tm
        tab_spec = pl.BlockSpec((tm, LANES), lambda j, i: (i % per_seq, 0))
        in_specs += [tab_spec, tab_spec]
        args += list(tables)
    return pl.pallas_call(
        functools.partial(_proj_kernel, modes=modes, has_tables=tables is not None),
        grid=(n // tn, m // tm),
        in_specs=in_specs,
        out_specs=pl.BlockSpec((tm, tn), lambda j, i: (i, j)),
        out_shape=jax.ShapeDtypeStruct((m, n), out_dtype),
        compiler_params=_cparams(("arbitrary", "arbitrary")),
        name="proj",
    )(*args)


def _rope_tables(seq_len, head_dim):
    half = head_dim // 2
    inv_freq = jnp.power(ROPE_THETA, -jnp.arange(half, dtype=F32) * 2.0 / head_dim)
    ang = jnp.arange(seq_len).astype(F32)[:, None] * inv_freq[None, :]
    cos, sin = jnp.cos(ang), jnp.sin(ang)
    reps = LANES // head_dim
    cos_t = jnp.tile(jnp.concatenate([cos, cos], axis=-1), (1, reps))
    sin_t = jnp.tile(jnp.concatenate([-sin, sin], axis=-1), (1, reps))
    return cos_t, sin_t


def _softmax_rows(s):
    m = jnp.max(s, axis=-1, keepdims=True)
    p = jnp.exp(s - m)
    l = jnp.sum(p, axis=-1, keepdims=True)
    return m, p, l


def _attn_a_kernel(first_ref, q_ref, kc_ref, vc_ref, kp_ref, vp_ref, o_ref, lse_ref, *, tq):
    c = pl.program_id(0)
    not_first = first_ref[c] == 0
    row = lax.broadcasted_iota(jnp.int32, (ATT_BLK, 2 * ATT_BLK), 0)
    col = lax.broadcasted_iota(jnp.int32, (ATT_BLK, 2 * ATT_BLK), 1)
    band = (col >= row) & (col <= row + N_BACK_A)
    band_first = band & ((col >= ATT_BLK) | not_first)
    for h in range(HEADS_PER_GROUP_A):
        hs = slice(h * HEAD_DIM_A, (h + 1) * HEAD_DIM_A)
        for i in range(tq // ATT_BLK):
            rows = slice(i * ATT_BLK, (i + 1) * ATT_BLK)
            q = q_ref[rows, hs]
            if i == 0:
                k = jnp.concatenate([kp_ref[:, hs], kc_ref[rows, hs]], axis=0)
                v = jnp.concatenate([vp_ref[:, hs], vc_ref[rows, hs]], axis=0)
                mask = band_first
            else:
                kv_rows = slice((i - 1) * ATT_BLK, (i + 1) * ATT_BLK)
                k = kc_ref[kv_rows, hs]
                v = vc_ref[kv_rows, hs]
                mask = band
            s = jnp.where(mask, _dot_nt(q, k), NEG)
            m, p, l = _softmax_rows(s)
            o = _dot(p.astype(BF16), v) / l
            o_ref[rows, hs] = o.astype(o_ref.dtype)
            lse_ref[rows, hs] = jnp.broadcast_to(m + jnp.log(l), (ATT_BLK, HEAD_DIM_A))


def _attn_a(qkv, first, tq):
    r = qkv.shape[0]
    gw = GROUP_WIDTH_A
    sub = tq // ATT_BLK
    cur = lambda col: pl.BlockSpec((tq, gw), lambda c, f: (c, col))
    prev = lambda col: pl.BlockSpec((ATT_BLK, gw), lambda c, f: (jnp.maximum(c * sub - 1, 0), col))
    return pl.pallas_call(
        functools.partial(_attn_a_kernel, tq=tq),
        grid_spec=pltpu.PrefetchScalarGridSpec(
            num_scalar_prefetch=1,
            grid=(r // tq,),
            in_specs=[cur(0), cur(1), cur(2), prev(1), prev(2)],
            out_specs=[pl.BlockSpec((tq, gw), lambda c, f: (c, 0)),
                       pl.BlockSpec((tq, gw), lambda c, f: (c, 0))],
        ),
        out_shape=[jax.ShapeDtypeStruct((r, gw), BF16), jax.ShapeDtypeStruct((r, gw), F32)],
        compiler_params=_cparams(("arbitrary",)),
        name="attn_a",
    )(first, qkv, qkv, qkv, qkv, qkv)


def _attn_b_kernel(sink_ref, q_ref, kvc_ref, kvp_ref, o_ref, *, tq, chunks_per_seq):
    c = pl.program_id(0)
    not_first = (c % chunks_per_seq) != 0
    row = lax.broadcasted_iota(jnp.int32, (ATT_BLK, 2 * ATT_BLK), 0)
    col = lax.broadcasted_iota(jnp.int32, (ATT_BLK, 2 * ATT_BLK), 1)
    band = (col >= row + 1) & (col <= row + 1 + N_BACK_B)
    band_first = band & ((col >= ATT_BLK) | not_first)
    lane_kv = lax.broadcasted_iota(jnp.int32, (2 * ATT_BLK, LANES), 1) < HEAD_DIM_B
    lane_o = lax.broadcasted_iota(jnp.int32, (ATT_BLK, LANES), 1) < HEAD_DIM_B
    pairs = (N_Q_HEADS_B // N_KV_HEADS_B) // 2
    for i in range(tq // ATT_BLK):
        rows = slice(i * ATT_BLK, (i + 1) * ATT_BLK)
        if i == 0:
            kv = jnp.concatenate([kvp_ref[...], kvc_ref[rows, :]], axis=0)
            mask = band_first
        else:
            kv = kvc_ref[(i - 1) * ATT_BLK:(i + 1) * ATT_BLK, :]
            mask = band
        for n in range(N_KV_HEADS_B):
            kk = kv[:, n * LANES:(n + 1) * LANES]
            vv = kv[:, (N_KV_HEADS_B + n) * LANES:(N_KV_HEADS_B + n + 1) * LANES]
            zero = jnp.zeros_like(kk)
            k_bd = jnp.concatenate([jnp.where(lane_kv, kk, zero), jnp.where(lane_kv, zero, kk)], axis=0)
            v_bd = jnp.concatenate([jnp.where(lane_kv, vv, zero), jnp.where(lane_kv, zero, vv)], axis=0)
            for jp in range(pairs):
                c0 = n * (WIDTH_QB // N_KV_HEADS_B) + jp * LANES
                q = q_ref[rows, c0:c0 + LANES]
                s2 = _dot_nt(q, k_bd)
                ps, invs = [], []
                for hh in range(2):
                    sink = sink_ref[n * 2 * pairs + jp * 2 + hh]
                    s = jnp.where(mask, s2[:, hh * 2 * ATT_BLK:(hh + 1) * 2 * ATT_BLK], NEG)
                    m = jnp.maximum(jnp.max(s, axis=-1, keepdims=True), sink)
                    p = jnp.exp(s - m)
                    denom = jnp.sum(p, axis=-1, keepdims=True) + jnp.exp(sink - m)
                    ps.append(p.astype(BF16))
                    invs.append(1.0 / denom)
                o = _dot(jnp.concatenate(ps, axis=1), v_bd)
                o = o * jnp.where(lane_o, invs[0], invs[1])
                o_ref[rows, c0:c0 + LANES] = o.astype(o_ref.dtype)


def _attn_b(qkv, sinks, seq_len, tq):
    t = qkv.shape[0]
    sub = tq // ATT_BLK
    kv_col = WIDTH_QB // (4 * LANES)
    return pl.pallas_call(
        functools.partial(_attn_b_kernel, tq=tq, chunks_per_seq=seq_len // tq),
        grid=(t // tq,),
        in_specs=[pl.BlockSpec(memory_space=pltpu.SMEM),
                  pl.BlockSpec((tq, WIDTH_QB), lambda c: (c, 0)),
                  pl.BlockSpec((tq, 4 * LANES), lambda c: (c, kv_col)),
                  pl.BlockSpec((ATT_BLK, 4 * LANES), lambda c: (jnp.maximum(c * sub - 1, 0), kv_col))],
        out_specs=pl.BlockSpec((tq, WIDTH_QB), lambda c: (c, 0)),
        out_shape=jax.ShapeDtypeStruct((t, WIDTH_QB), BF16),
        compiler_params=_cparams(("arbitrary",)),
        name="attn_b",
    )(sinks, qkv, qkv, qkv)


def _attn_c_kernel(q_ref, mkv_ref, o_ref, *, tq, sub):
    for h in range(N_HEADS_C):
        hs = slice(h * HEAD_DIM_C, (h + 1) * HEAD_DIM_C)
        mk = mkv_ref[:, hs]
        mv = mkv_ref[:, WIDTH_C + h * HEAD_DIM_C:WIDTH_C + (h + 1) * HEAD_DIM_C]
        for i in range(tq // sub):
            rows = slice(i * sub, (i + 1) * sub)
            s = _dot_nt(q_ref[rows, hs], mk)
            _, p, l = _softmax_rows(s)
            o_ref[rows, hs] = (_dot(p.astype(BF16), mv) / l).astype(o_ref.dtype)


def _attn_c(qc, mkv, seq_len, n_mem, tq):
    t = qc.shape[0]
    per_seq = seq_len // tq
    return pl.pallas_call(
        functools.partial(_attn_c_kernel, tq=tq, sub=min(tq, 256)),
        grid=(t // tq,),
        in_specs=[pl.BlockSpec((tq, WIDTH_C), lambda c: (c, 0)),
                  pl.BlockSpec((n_mem, 2 * WIDTH_C), lambda c: (c // per_seq, 0))],
        out_specs=pl.BlockSpec((tq, WIDTH_C), lambda c: (c, 0)),
        out_shape=jax.ShapeDtypeStruct((t, WIDTH_C), BF16),
        compiler_params=_cparams(("arbitrary",)),
        name="attn_c",
    )(qc, mkv)


def _layer_norm(z, g, b):
    mu = jnp.mean(z, axis=-1, keepdims=True)
    zc = z - mu
    var = jnp.mean(zc * zc, axis=-1, keepdims=True)
    return zc * lax.rsqrt(var + LN_EPS) * g + b


def _merge_kernel(oa_ref, lse_ref, yb_ref, yc_ref, gate_ref, x_ref, wa_ref, wb_ref, wc_ref, wo_ref,
                  g_ref, b_ref, h_ref, *, d_model):
    lses = [lse_ref[g] for g in range(len(DILATIONS_A))]
    top = functools.reduce(jnp.maximum, lses)
    es = [jnp.exp(l - top) for l in lses]
    tot = functools.reduce(jnp.add, es)
    ya = functools.reduce(jnp.add, [e * oa_ref[g].astype(F32) for g, e in enumerate(es)]) / tot
    gate = lambda g: gate_ref[:, g * d_model:(g + 1) * d_model].astype(F32)
    merged = gate(0) * _dot(ya.astype(BF16), wa_ref[...])
    merged += gate(1) * _dot(yb_ref[...], wb_ref[...])
    merged += gate(2) * _dot(yc_ref[...], wc_ref[...])
    mix = _dot(merged.astype(BF16), wo_ref[...])
    h_ref[...] = _layer_norm(ALPHA * x_ref[...] + mix, g_ref[...], b_ref[...])


def _merge(oa, lse, yb, yc, gates, x2, wa, wb, wc, wo, ln_g, ln_b, tm):
    t, d_model = x2.shape
    row = lambda w: pl.BlockSpec((tm, w), lambda i: (i, 0))
    grp = lambda w: pl.BlockSpec((len(DILATIONS_A), tm, w), lambda i: (0, i, 0))
    const = lambda a: pl.BlockSpec(a.shape, lambda i: (0, 0), pipeline_mode=pl.Buffered(1))
    return pl.pallas_call(
        functools.partial(_merge_kernel, d_model=d_model),
        grid=(t // tm,),
        in_specs=[grp(GROUP_WIDTH_A), grp(GROUP_WIDTH_A), row(WIDTH_QB), row(WIDTH_C),
                  row(3 * d_model), row(d_model), const(wa), const(wb), const(wc), const(wo),
                  const(ln_g), const(ln_b)],
        out_specs=row(d_model),
        out_shape=jax.ShapeDtypeStruct((t, d_model), F32),
        compiler_params=_cparams(("arbitrary",)),
        name="merge",
    )(oa, lse, yb, yc, gates, x2, wa, wb, wc, wo, ln_g, ln_b)


def _router_kernel(h_ref, whi_ref, wlo_ref, bias_ref, idx_ref, wgt_ref, cnt_ref, *, n_experts):
    h = h_ref[...]
    h_hi = h.astype(BF16)
    h_lo = (h - h_hi.astype(F32)).astype(BF16)
    logits = _dot_nt(whi_ref[...], h_hi) + _dot_nt(whi_ref[...], h_lo) + _dot_nt(wlo_ref[...], h_hi)
    scores = jax.nn.sigmoid(logits)
    biased = scores + bias_ref[...]
    tm = scores.shape[1]
    per_group = n_experts // N_EXPERT_GROUPS
    neg_inf = -jnp.inf

    b3 = biased.reshape(N_EXPERT_GROUPS, per_group, tm)
    i3 = lax.broadcasted_iota(jnp.int32, b3.shape, 1)
    m1 = jnp.max(b3, axis=1, keepdims=True)
    a1 = jnp.min(jnp.where(b3 == m1, i3, per_group), axis=1, keepdims=True)
    m2 = jnp.max(jnp.where(i3 == a1, neg_inf, b3), axis=1, keepdims=True)
    gscore = (m1 + m2).reshape(N_EXPERT_GROUPS, tm)

    gi = lax.broadcasted_iota(jnp.int32, gscore.shape, 0)
    chosen = jnp.zeros(gscore.shape, jnp.int32)
    for _ in range(TOPK_EXPERT_GROUPS):
        gm = jnp.max(gscore, axis=0, keepdims=True)
        ga = jnp.min(jnp.where(gscore == gm, gi, N_EXPERT_GROUPS), axis=0, keepdims=True)
        hit = gi == ga
        chosen = jnp.where(hit, 1, chosen)
        gscore = jnp.where(hit, neg_inf, gscore)

    cur = jnp.where(chosen.reshape(N_EXPERT_GROUPS, 1, tm) > 0, b3, neg_inf).reshape(n_experts, tm)
    ei = lax.broadcasted_iota(jnp.int32, cur.shape, 0)
    idxs, wgts = [], []
    member = jnp.zeros(cur.shape, F32)
    for _ in range(TOP_K):
        m = jnp.max(cur, axis=0, keepdims=True)
        a = jnp.min(jnp.where(cur == m, ei, n_experts), axis=0, keepdims=True)
        hit = ei == a
        idxs.append(a)
        wgts.append(jnp.sum(jnp.where(hit, scores, 0.0), axis=0, keepdims=True))
        member = jnp.where(hit, 1.0, member)
        cur = jnp.where(hit, neg_inf, cur)
    wsum = functools.reduce(jnp.add, wgts)
    idx_ref[...] = jnp.concatenate(idxs, axis=0)
    wgt_ref[...] = jnp.concatenate(wgts, axis=0) / wsum * ROUTED_SCALE

    @pl.when(pl.program_id(0) == 0)
    def _():
        cnt_ref[...] = jnp.zeros_like(cnt_ref)

    cnt_ref[...] += functools.reduce(
        jnp.add, [member[:, c * LANES:(c + 1) * LANES] for c in range(tm // LANES)])


def _router(h1, w_hi, w_lo, bias, tm):
    t, d_model = h1.shape
    n_experts = w_hi.shape[0]
    const = lambda a: pl.BlockSpec(a.shape, lambda i: (0, 0))
    return pl.pallas_call(
        functools.partial(_router_kernel, n_experts=n_experts),
        grid=(t // tm,),
        in_specs=[pl.BlockSpec((tm, d_model), lambda i: (i, 0)), const(w_hi), const(w_lo), const(bias)],
        out_specs=[pl.BlockSpec((TOP_K, tm), lambda i: (0, i)),
                   pl.BlockSpec((TOP_K, tm), lambda i: (0, i)),
                   pl.BlockSpec((n_experts, LANES), lambda i: (0, 0))],
        out_shape=[jax.ShapeDtypeStruct((TOP_K, t), jnp.int32), jax.ShapeDtypeStruct((TOP_K, t), F32),
                   jax.ShapeDtypeStruct((n_experts, LANES), F32)],
        compiler_params=_cparams(("arbitrary",)),
        name="router",
    )(h1, w_hi, w_lo, bias)


def _rank_kernel(idx_ref, base_ref, dest_ref, carry, *, n_experts):
    @pl.when(pl.program_id(0) == 0)
    def _():
        carry[...] = jnp.zeros_like(carry)

    idx = idx_ref[...]
    tm = idx.shape[1]
    ei = lax.broadcasted_iota(jnp.int32, (n_experts, tm), 0)
    hits = [ei == idx[k:k + 1, :] for k in range(TOP_K)]
    member = functools.reduce(jnp.add, [jnp.where(h, 1.0, 0.0) for h in hits])
    earlier = (lax.broadcasted_iota(jnp.int32, (tm, tm), 0)
               < lax.broadcasted_iota(jnp.int32, (tm, tm), 1))
    before = _dot(member.astype(BF16), jnp.where(earlier, 1.0, 0.0).astype(BF16))
    row = base_ref[...] + carry[...] + before
    dest = [jnp.sum(jnp.where(h, row, 0.0), axis=0, keepdims=True) for h in hits]
    dest_ref[...] = jnp.concatenate(dest, axis=0).astype(jnp.int32)
    carry[...] += jnp.sum(member, axis=1, keepdims=True)


def _ranks(top_idx, base, tm):
    k, t = top_idx.shape
    n_experts = base.shape[0]
    return pl.pallas_call(
        functools.partial(_rank_kernel, n_experts=n_experts),
        grid=(t // tm,),
        in_specs=[pl.BlockSpec((k, tm), lambda i: (0, i)), pl.BlockSpec((n_experts, 1), lambda i: (0, 0))],
        out_specs=pl.BlockSpec((k, tm), lambda i: (0, i)),
        out_shape=jax.ShapeDtypeStruct((k, t), jnp.int32),
        scratch_shapes=[pltpu.VMEM((n_experts, 1), F32)],
        compiler_params=_cparams(("arbitrary",)),
        name="ranks",
    )(top_idx, base)


def _expert_kernel(blk0_ref, nblk_ref, tab_hbm, h_hbm, wg_ref, wu_ref, wd_ref, out_hbm,
                   tab, wgu_s, wd_s, xb_s, xbuf, ybuf, gsem, ssem, tsem, *, n_tok, d_expert):
    n_real_rows = TOP_K * n_tok
    dc = d_expert // EXPERT_PARTS
    rows_per_part = MOE_ROWS // EXPERT_PARTS
    e = pl.program_id(0)
    nb = nblk_ref[e]
    b0 = blk0_ref[e]

    def tab_copy(row):
        slot = row & (TAB_RING - 1)
        return pltpu.make_async_copy(tab_hbm.at[pl.ds(row, 1), :], tab.at[pl.ds(slot, 1), :], tsem.at[slot])

    def start_gather(g, buf, r0, n_rows):
        for rr in range(n_rows):
            tok = tab[(g + 1) & (TAB_RING - 1), r0 + rr] & (n_tok - 1)
            pltpu.make_async_copy(h_hbm.at[pl.ds(tok, 1), :], xbuf.at[buf, pl.ds(r0 + rr, 1), :],
                                  gsem.at[buf]).start(priority=rr % 2)

    def start_scatter(g, buf, r0, n_rows):
        for rr in range(n_rows):
            pltpu.make_async_copy(ybuf.at[buf, pl.ds(r0 + rr, 1), :],
                                  out_hbm.at[pl.ds(tab[(g + 1) & (TAB_RING - 1), r0 + rr], 1), :],
                                  ssem.at[buf]).start(priority=rr % 2)

    def start_all(start, g, buf):
        def group(i, _):
            start(g, buf, i * ISSUE_GROUP, ISSUE_GROUP)
            return 0

        lax.fori_loop(0, MOE_ROWS // ISSUE_GROUP, group, 0)

    def wait_gather(buf):
        pltpu.make_async_copy(h_hbm.at[pl.ds(0, MOE_ROWS), :], xbuf.at[buf], gsem.at[buf]).wait()

    def wait_scatter(buf):
        pltpu.make_async_copy(ybuf.at[buf], out_hbm.at[pl.ds(0, MOE_ROWS), :], ssem.at[buf]).wait()

    @pl.when(e == 0)
    def _():
        for row in range(3):
            tab_copy(row).start()
        ybuf[...] = jnp.zeros_like(ybuf)
        tab_copy(0).wait()
        tab_copy(1).wait()

        def prime(i, _):
            for rr in range(ISSUE_GROUP):
                r = i * ISSUE_GROUP + rr
                pltpu.make_async_copy(ybuf.at[0, pl.ds(r, 1), :], out_hbm.at[pl.ds(n_real_rows + r, 1), :],
                                      ssem.at[0]).start(priority=rr % 2)
            return 0

        lax.fori_loop(0, MOE_ROWS // ISSUE_GROUP, prime, 0)
        start_all(start_gather, 0, 0)

    @pl.when(nb > 0)
    def _():
        for j in range(EXPERT_PARTS):
            cs = slice(j * dc, (j + 1) * dc)
            wgu_s[j, :, :dc] = wg_ref[0, :, cs].astype(BF16)
            wgu_s[j, :, dc:] = wu_ref[0, :, cs].astype(BF16)
            wd_s[j] = wd_ref[0, cs, :].astype(BF16)

        def block(b, _):
            g = b0 + b
            buf = g & 1
            tab_copy(g + 3).start()
            tab_copy(g + 2).wait()
            wait_gather(buf)
            wait_scatter(buf)
            xb_s[...] = xbuf[buf].astype(BF16)
            ybuf[buf] = jnp.zeros((MOE_ROWS, ybuf.shape[2]), F32)

            def part(j, _):
                r0 = j * rows_per_part
                start_gather(g + 1, 1 - buf, r0, rows_per_part)
                start_scatter(g - 1, 1 - buf, r0, rows_per_part)
                gu = _dot(xb_s[...], wgu_s[j])
                act = gu[:, :dc]
                hid = (act * jax.nn.sigmoid(act) * gu[:, dc:]).astype(BF16)
                ybuf[buf] += _dot(hid, wd_s[j])
                return 0

            lax.fori_loop(0, EXPERT_PARTS, part, 0)
            return 0

        lax.fori_loop(0, nb, block, 0)

    @pl.when(e == pl.num_programs(0) - 1)
    def _():
        n_act = b0 + nb
        last = (n_act - 1) & 1
        start_all(start_scatter, n_act - 1, last)
        tab_copy(n_act + 2).wait()
        wait_scatter(1 - last)
        wait_scatter(last)
        wait_gather(1 - last)


def _experts(blk0, nblk, dest, h1, w_gate, w_up, w_down):
    n_tok, d_model = h1.shape
    n_experts, _, d_expert = w_gate.shape
    n_assign = TOP_K * n_tok
    n_blocks = n_assign // MOE_ROWS + n_experts + 2
    assert n_tok & (n_tok - 1) == 0
    rows = jnp.arange((n_blocks + 1) * MOE_ROWS, dtype=jnp.int32)
    pad = n_assign + (((rows // MOE_ROWS) + 1) % 2) * MOE_ROWS + rows % MOE_ROWS
    tab = pad.at[dest.reshape(n_assign) + MOE_ROWS].set(jnp.arange(n_assign, dtype=jnp.int32),
                                                        unique_indices=True)
    return pl.pallas_call(
        functools.partial(_expert_kernel, n_tok=n_tok, d_expert=d_expert),
        grid_spec=pltpu.PrefetchScalarGridSpec(
            num_scalar_prefetch=2,
            grid=(n_experts,),
            in_specs=[pl.BlockSpec(memory_space=pl.ANY),
                      pl.BlockSpec(memory_space=pl.ANY),
                      pl.BlockSpec((1, d_model, d_expert), lambda e, *_: (e, 0, 0)),
                      pl.BlockSpec((1, d_model, d_expert), lambda e, *_: (e, 0, 0)),
                      pl.BlockSpec((1, d_expert, d_model), lambda e, *_: (e, 0, 0))],
            out_specs=pl.BlockSpec(memory_space=pl.ANY),
            scratch_shapes=[pltpu.SMEM((TAB_RING, MOE_ROWS), jnp.int32),
                            pltpu.VMEM((EXPERT_PARTS, d_model, 2 * d_expert // EXPERT_PARTS), BF16),
                            pltpu.VMEM((EXPERT_PARTS, d_expert // EXPERT_PARTS, d_model), BF16),
                            pltpu.VMEM((MOE_ROWS, d_model), BF16),
                            pltpu.VMEM((2, MOE_ROWS, d_model), F32),
                            pltpu.VMEM((2, MOE_ROWS, d_model), F32),
                            pltpu.SemaphoreType.DMA((2,)),
                            pltpu.SemaphoreType.DMA((2,)),
                            pltpu.SemaphoreType.DMA((TAB_RING,))],
        ),
        out_shape=jax.ShapeDtypeStruct((n_assign + 2 * MOE_ROWS, d_model), F32),
        compiler_params=_cparams(("arbitrary",)),
        name="experts",
    )(blk0, nblk, tab.reshape(n_blocks + 1, MOE_ROWS), h1, w_gate, w_up, w_down)


def _final_kernel(*refs, d_expert):
    h_ref, wt_ref = refs[0], refs[1]
    routed_refs = refs[2:2 + TOP_K]
    wgu_ref, wd_ref, g_ref, b_ref, o_ref = refs[2 + TOP_K:]
    h = h_ref[...]
    gu = _dot(h.astype(BF16), wgu_ref[...])
    act = gu[:, :d_expert]
    hid = (act * jax.nn.sigmoid(act) * gu[:, d_expert:]).astype(BF16)
    ff = _dot(hid, wd_ref[...])
    wt = wt_ref[...]
    for k in range(TOP_K):
        ff += routed_refs[k][...] * wt[:, k:k + 1]
    o_ref[...] = _layer_norm(ALPHA * h + ff, g_ref[...], b_ref[...])


def _final(h1, wt, routed, w_gu, w_d, ln_g, ln_b, tm):
    t, d_model = h1.shape
    d_expert = w_d.shape[0]
    per_k = t // tm
    row = lambda w: pl.BlockSpec((tm, w), lambda i: (i, 0))
    const = lambda a: pl.BlockSpec(a.shape, lambda i: (0, 0))
    routed_specs = [pl.BlockSpec((tm, d_model), lambda i, k=k: (k * per_k + i, 0)) for k in range(TOP_K)]
    return pl.pallas_call(
        functools.partial(_final_kernel, d_expert=d_expert),
        grid=(t // tm,),
        in_specs=[row(d_model), row(TOP_K)] + routed_specs + [const(w_gu), const(w_d), const(ln_g), const(ln_b)],
        out_specs=row(d_model),
        out_shape=jax.ShapeDtypeStruct((t, d_model), F32),
        compiler_params=_cparams(("arbitrary",)),
        name="final",
    )(h1, wt, *([routed] * TOP_K), w_gu, w_d, ln_g, ln_b)


def _to_subsequences(a, bsz, seq_len, d):
    c = a.shape[1]
    return a.reshape(bsz, seq_len // d, d, c).transpose(0, 2, 1, 3).reshape(bsz * seq_len, c)


def _from_subsequences(a, bsz, seq_len, d):
    c = a.shape[1]
    return a.reshape(bsz, d, seq_len // d, c).transpose(0, 2, 1, 3).reshape(bsz * seq_len, c)


def _layer(h, mem, w_in, w_mem_kv, sinks, w_a, w_b, w_c, w_out, ln1_g, ln1_b, router_w, router_bias,
           w_eg, w_eu, w_ed, w_sg, w_su, w_sd, ln2_g, ln2_b):
    bsz, seq_len, d_model = h.shape
    n_tok = bsz * seq_len
    n_mem = mem.shape[1]
    n_experts = router_w.shape[1]
    x2 = h.reshape(n_tok, d_model)
    xb = x2.astype(BF16)

    o_qa, o_ka, o_va = 0, WIDTH_A, 2 * WIDTH_A
    o_qb = 3 * WIDTH_A
    o_kb = o_qb + WIDTH_QB
    o_vb = o_kb + WIDTH_KVB
    o_qc = o_vb + WIDTH_KVB
    o_gate = o_qc + WIDTH_C
    wb16 = w_in.astype(BF16)
    cols = lambda o, n: wb16[:, o:o + n]
    chunks_a = WIDTH_A // LANES
    scale_a = HEAD_DIM_A ** -0.5
    tabs_a = _rope_tables(seq_len, HEAD_DIM_A)
    qkv_a = _proj(xb, cols(0, 3 * WIDTH_A),
                  ((("rope128", scale_a),) * chunks_a, (("rope128", 1.0),) * chunks_a,
                   (("plain", 1.0),) * chunks_a),
                  WIDTH_A, BF16, tabs_a, seq_len)

    hb = HEAD_DIM_B
    dup = lambda o: [cols(o + n * hb, hb) for n in range(N_KV_HEADS_B) for _ in range(2)]
    w_qkv_b = jnp.concatenate([cols(o_qb, WIDTH_QB)] + dup(o_kb) + dup(o_vb), axis=1)
    q_chunks = WIDTH_QB // LANES
    modes_b = ((("rope64", HEAD_DIM_B ** -0.5),) * q_chunks + (("rope64", 1.0),) * N_KV_HEADS_B
               + (("plain", 1.0),) * N_KV_HEADS_B,)
    qkv_b = _proj(xb, w_qkv_b, modes_b, w_qkv_b.shape[1], BF16, _rope_tables(seq_len, HEAD_DIM_B), seq_len)

    q_c = _proj(xb, cols(o_qc, WIDTH_C), ((("plain", HEAD_DIM_C ** -0.5),) * (WIDTH_C // LANES),),
                WIDTH_C, BF16)
    gate_tn = 1536
    gates = _proj(xb, cols(o_gate, 3 * d_model),
                  ((("sigmoid", 1.0),) * (gate_tn // LANES),) * (3 * d_model // gate_tn), gate_tn, BF16)
    mkv = _proj(mem.reshape(bsz * n_mem, d_model).astype(BF16), w_mem_kv.astype(BF16),
                ((("plain", 1.0),) * (2 * WIDTH_C // LANES),), 2 * WIDTH_C, BF16)

    tq = min(512, seq_len // max(DILATIONS_A))
    parts, firsts = [], []
    for gi, d in enumerate(DILATIONS_A):
        grp = jnp.concatenate([qkv_a[:, o + gi * GROUP_WIDTH_A:o + (gi + 1) * GROUP_WIDTH_A]
                               for o in (o_qa, o_ka, o_va)], axis=1)
        parts.append(_to_subsequences(grp, bsz, seq_len, d))
        sub_chunks = seq_len // d // tq
        firsts.append((jnp.arange(n_tok // tq, dtype=jnp.int32) % sub_chunks == 0).astype(jnp.int32))
    o_sub, lse_sub = _attn_a(jnp.concatenate(parts, axis=0), jnp.concatenate(firsts), tq)
    oa = jnp.stack([_from_subsequences(o_sub[gi * n_tok:(gi + 1) * n_tok], bsz, seq_len, d)
                    for gi, d in enumerate(DILATIONS_A)])
    lse = jnp.stack([_from_subsequences(lse_sub[gi * n_tok:(gi + 1) * n_tok], bsz, seq_len, d)
                     for gi, d in enumerate(DILATIONS_A)])

    yb = _attn_b(qkv_b, sinks.astype(F32), seq_len, tq)
    yc = _attn_c(q_c, mkv, seq_len, n_mem, tq)

    h1 = _merge(oa, lse, yb, yc, gates, x2, w_a.astype(BF16), w_b.astype(BF16), w_c.astype(BF16),
                w_out.astype(BF16), ln1_g.reshape(1, d_model), ln1_b.reshape(1, d_model), tm=256)

    rw_t = router_w.T
    rw_hi = rw_t.astype(BF16)
    rw_lo = (rw_t - rw_hi.astype(F32)).astype(BF16)
    top_idx, top_w, cnt = _router(h1, rw_hi, rw_lo, router_bias.reshape(n_experts, 1).astype(F32), tm=256)
    counts = jnp.sum(cnt, axis=1).astype(jnp.int32)
    nblk = (counts + MOE_ROWS - 1) // MOE_ROWS
    blk0 = jnp.cumsum(nblk) - nblk
    dest = _ranks(top_idx, (blk0 * MOE_ROWS).astype(F32).reshape(n_experts, 1), tm=256)
    routed = _experts(blk0, nblk, dest, h1, w_eg, w_eu, w_ed)
    w_sgu = jnp.concatenate([w_sg, w_su], axis=1).astype(BF16)
    out = _final(h1, top_w.T, routed, w_sgu, w_sd.astype(BF16), ln2_g.reshape(1, d_model),
                 ln2_b.reshape(1, d_model), tm=256)
    return out.reshape(bsz, seq_len, d_model)


def kernel(x, mem, w_in, w_mem_kv, attn_sinks, w_branch_a, w_branch_b, w_branch_c, w_out, ln1_g, ln1_b,
           router_w, router_bias, w_exp_gate, w_exp_up, w_exp_down, w_sh_gate, w_sh_up, w_sh_down,
           ln2_g, ln2_b):
    h = x
    for layer in range(DEPTH):
        h = _layer(h, mem, w_in[layer], w_mem_kv[layer], attn_sinks[layer], w_branch_a[layer],
                   w_branch_b[layer], w_branch_c[layer], w_out[layer], ln1_g[layer], ln1_b[layer],
                   router_w[layer], router_bias[layer], w_exp_gate[layer], w_exp_up[layer],
                   w_exp_down[layer], w_sh_gate[layer], w_sh_up[layer], w_sh_down[layer],
                   ln2_g[layer], ln2_b[layer])
    return h
```

```python
import functools

import jax
import jax.numpy as jnp
from jax import lax
from jax.experimental import pallas as pl
from jax.experimental.pallas import tpu as pltpu

F32 = jnp.float32
BF16 = jnp.bfloat16

DEPTH = 1
HEAD_DIM_A = 128
DILATIONS_A = (1, 4, 16)
N_BACK_A = 128
HEADS_PER_GROUP_A = 4
GROUP_WIDTH_A = HEADS_PER_GROUP_A * HEAD_DIM_A
WIDTH_A = len(DILATIONS_A) * GROUP_WIDTH_A
HEAD_DIM_B = 64
N_Q_HEADS_B = 16
N_KV_HEADS_B = 2
WIDTH_QB = N_Q_HEADS_B * HEAD_DIM_B
WIDTH_KVB = N_KV_HEADS_B * HEAD_DIM_B
N_BACK_B = 127
N_HEADS_C = 4
HEAD_DIM_C = 256
WIDTH_C = N_HEADS_C * HEAD_DIM_C
ROPE_THETA = 10000.0
TOP_K = 8
N_EXPERT_GROUPS = 8
TOPK_EXPERT_GROUPS = 4
ROUTED_SCALE = 2.5
ALPHA = (2 * DEPTH) ** 0.25
LN_EPS = 1e-5

LANES = 128
ATT_BLK = 128
MOE_ROWS = 128
TAB_RING = 4
VMEM_LIMIT = 56 * 1024 * 1024
NEG = -1e30


def _cparams(sem):
    return pltpu.CompilerParams(dimension_semantics=sem, vmem_limit_bytes=VMEM_LIMIT)


def _dot(a, b):
    return jnp.dot(a, b, preferred_element_type=F32)


def _dot_nt(a, b):
    return lax.dot_general(a, b, (((1,), (1,)), ((), ())), preferred_element_type=F32)


def _rope_chunk(a, cos, sin, head_dim):
    if head_dim == 128:
        partner = pltpu.roll(a, 64, 1)
    else:
        lane = lax.broadcasted_iota(jnp.int32, a.shape, 1)
        partner = jnp.where((lane % 64) < 32, pltpu.roll(a, 96, 1), pltpu.roll(a, 32, 1))
    return a * cos + partner * sin


def _proj_kernel(*refs, modes, has_tables, dilations):
    n_out = 1 if dilations is None else len(dilations)
    x_ref, w_ref = refs[:2]
    cos_ref, sin_ref = refs[2:4] if has_tables else (None, None)
    o_refs = refs[2 + 2 * has_tables:2 + 2 * has_tables + n_out]
    stage = refs[-1] if dilations is not None else None
    j = pl.program_id(0)
    acc = _dot(x_ref[...], w_ref[...])
    tm = acc.shape[0]

    def store(c, a):
        if dilations is None:
            o_refs[0][:, c * LANES:(c + 1) * LANES] = a.astype(o_refs[0].dtype)
            return
        per_group = len(modes[0]) // n_out
        gi, cg = divmod(c, per_group)
        d, o_ref = dilations[gi], o_refs[gi]
        lanes = slice(cg * LANES, (cg + 1) * LANES)
        if d == 1:
            o_ref[0, 0, :, lanes] = a.astype(o_ref.dtype)
        else:
            stage[c] = a
            for r in range(d):
                o_ref[0, r, :, lanes] = stage[c, pl.ds(r, tm // d, stride=d), :].astype(o_ref.dtype)

    def epilogue(chunk_modes):
        for c, (kind, scale) in enumerate(chunk_modes):
            a = acc[:, c * LANES:(c + 1) * LANES]
            if kind == "rope128":
                a = _rope_chunk(a, cos_ref[...], sin_ref[...], 128)
            elif kind == "rope64":
                a = _rope_chunk(a, cos_ref[...], sin_ref[...], 64)
            elif kind == "sigmoid":
                a = jax.nn.sigmoid(a)
            if scale != 1.0:
                a = a * scale
            store(c, a)

    if all(m == modes[0] for m in modes):
        epilogue(modes[0])
    else:
        for jj, chunk_modes in enumerate(modes):
            pl.when(j == jj)(functools.partial(epilogue, chunk_modes))


def _proj(x, w, modes, tn, out_dtype, tables=None, seq_len=None, tm=1024, dilations=None):
    m, k = x.shape
    n = w.shape[1]
    tm = min(tm, m)
    assert m % tm == 0 and n % tn == 0 and len(modes) == n // tn
    in_specs = [pl.BlockSpec((tm, k), lambda j, i: (i, 0)),
                pl.BlockSpec((k, tn), lambda j, i: (0, j))]
    args = [x, w]
    if tables is not None:
        per_seq = seq_len // tm
        tab_spec = pl.BlockSpec((tm, LANES), lambda j, i: (i % per_seq, 0))
        in_specs += [tab_spec, tab_spec]
        args += list(tables)
    scratch = []
    if dilations is None:
        out_specs = pl.BlockSpec((tm, tn), lambda j, i: (i, j))
        out_shape = jax.ShapeDtypeStruct((m, n), out_dtype)
    else:
        per_seq = seq_len // tm
        gw = tn // len(dilations)
        out_specs = [pl.BlockSpec((1, d, tm // d, gw), lambda j, i: (i // per_seq, 0, i % per_seq, j))
                     for d in dilations]
        out_shape = [jax.ShapeDtypeStruct((m // seq_len, d, seq_len // d, (n // tn) * gw), out_dtype)
                     for d in dilations]
        scratch = [pltpu.VMEM((tn // LANES, tm, LANES), F32)]
    return pl.pallas_call(
        functools.partial(_proj_kernel, modes=modes, has_tables=tables is not None, dilations=dilations),
        grid=(n // tn, m // tm),
        in_specs=in_specs,
        out_specs=out_specs,
        out_shape=out_shape,
        scratch_shapes=scratch,
        compiler_params=_cparams(("arbitrary", "arbitrary")),
        name="proj",
    )(*args)


def _rope_tables(seq_len, head_dim):
    half = head_dim // 2
    inv_freq = jnp.power(ROPE_THETA, -jnp.arange(half, dtype=F32) * 2.0 / head_dim)
    ang = jnp.arange(seq_len).astype(F32)[:, None] * inv_freq[None, :]
    cos, sin = jnp.cos(ang), jnp.sin(ang)
    reps = LANES // head_dim
    cos_t = jnp.tile(jnp.concatenate([cos, cos], axis=-1), (1, reps))
    sin_t = jnp.tile(jnp.concatenate([-sin, sin], axis=-1), (1, reps))
    return cos_t, sin_t


def _softmax_rows(s):
    m = jnp.max(s, axis=-1, keepdims=True)
    p = jnp.exp(s - m)
    l = jnp.sum(p, axis=-1, keepdims=True)
    return m, p, l


def _attn_a_kernel(first_ref, q_ref, kc_ref, vc_ref, kp_ref, vp_ref, o_ref, lse_ref, *, tq):
    c = pl.program_id(0)
    not_first = first_ref[c] == 0
    row = lax.broadcasted_iota(jnp.int32, (ATT_BLK, 2 * ATT_BLK), 0)
    col = lax.broadcasted_iota(jnp.int32, (ATT_BLK, 2 * ATT_BLK), 1)
    band = (col >= row) & (col <= row + N_BACK_A)
    band_first = band & ((col >= ATT_BLK) | not_first)
    for h in range(HEADS_PER_GROUP_A):
        hs = slice(h * HEAD_DIM_A, (h + 1) * HEAD_DIM_A)
        for i in range(tq // ATT_BLK):
            rows = slice(i * ATT_BLK, (i + 1) * ATT_BLK)
            q = q_ref[rows, hs]
            if i == 0:
                k = jnp.concatenate([kp_ref[:, hs], kc_ref[rows, hs]], axis=0)
                v = jnp.concatenate([vp_ref[:, hs], vc_ref[rows, hs]], axis=0)
                mask = band_first
            else:
                kv_rows = slice((i - 1) * ATT_BLK, (i + 1) * ATT_BLK)
                k = kc_ref[kv_rows, hs]
                v = vc_ref[kv_rows, hs]
                mask = band
            s = jnp.where(mask, _dot_nt(q, k), NEG)
            m, p, l = _softmax_rows(s)
            o = _dot(p.astype(BF16), v) / l
            o_ref[rows, hs] = o.astype(o_ref.dtype)
            lse_ref[rows, hs] = jnp.broadcast_to(m + jnp.log(l), (ATT_BLK, HEAD_DIM_A))


def _attn_a(qkv, first, tq):
    r = qkv.shape[0]
    gw = GROUP_WIDTH_A
    sub = tq // ATT_BLK
    cur = lambda col: pl.BlockSpec((tq, gw), lambda c, f: (c, col))
    prev = lambda col: pl.BlockSpec((ATT_BLK, gw), lambda c, f: (jnp.maximum(c * sub - 1, 0), col))
    return pl.pallas_call(
        functools.partial(_attn_a_kernel, tq=tq),
        grid_spec=pltpu.PrefetchScalarGridSpec(
            num_scalar_prefetch=1,
            grid=(r // tq,),
            in_specs=[cur(0), cur(1), cur(2), prev(1), prev(2)],
            out_specs=[pl.BlockSpec((tq, gw), lambda c, f: (c, 0)),
                       pl.BlockSpec((tq, gw), lambda c, f: (c, 0))],
        ),
        out_shape=[jax.ShapeDtypeStruct((r, gw), BF16), jax.ShapeDtypeStruct((r, gw), F32)],
        compiler_params=_cparams(("arbitrary",)),
        name="attn_a",
    )(first, qkv, qkv, qkv, qkv, qkv)


def _attn_b_kernel(sink_ref, q_ref, kvc_ref, kvp_ref, o_ref, *, tq, chunks_per_seq):
    c = pl.program_id(0)
    not_first = (c % chunks_per_seq) != 0
    row = lax.broadcasted_iota(jnp.int32, (ATT_BLK, 2 * ATT_BLK), 0)
    col = lax.broadcasted_iota(jnp.int32, (ATT_BLK, 2 * ATT_BLK), 1)
    band = (col >= row + 1) & (col <= row + 1 + N_BACK_B)
    band_first = band & ((col >= ATT_BLK) | not_first)
    lane_kv = lax.broadcasted_iota(jnp.int32, (2 * ATT_BLK, LANES), 1) < HEAD_DIM_B
    lane_o = lax.broadcasted_iota(jnp.int32, (ATT_BLK, LANES), 1) < HEAD_DIM_B
    pairs = (N_Q_HEADS_B // N_KV_HEADS_B) // 2
    for i in range(tq // ATT_BLK):
        rows = slice(i * ATT_BLK, (i + 1) * ATT_BLK)
        if i == 0:
            kv = jnp.concatenate([kvp_ref[...], kvc_ref[rows, :]], axis=0)
            mask = band_first
        else:
            kv = kvc_ref[(i - 1) * ATT_BLK:(i + 1) * ATT_BLK, :]
            mask = band
        for n in range(N_KV_HEADS_B):
            kk = kv[:, n * LANES:(n + 1) * LANES]
            vv = kv[:, (N_KV_HEADS_B + n) * LANES:(N_KV_HEADS_B + n + 1) * LANES]
            zero = jnp.zeros_like(kk)
            k_bd = jnp.concatenate([jnp.where(lane_kv, kk, zero), jnp.where(lane_kv, zero, kk)], axis=0)
            v_bd = jnp.concatenate([jnp.where(lane_kv, vv, zero), jnp.where(lane_kv, zero, vv)], axis=0)
            for jp in range(pairs):
                c0 = n * (WIDTH_QB // N_KV_HEADS_B) + jp * LANES
                q = q_ref[rows, c0:c0 + LANES]
                s2 = _dot_nt(q, k_bd)
                ps, invs = [], []
                for hh in range(2):
                    sink = sink_ref[n * 2 * pairs + jp * 2 + hh]
                    s = jnp.where(mask, s2[:, hh * 2 * ATT_BLK:(hh + 1) * 2 * ATT_BLK], NEG)
                    m = jnp.maximum(jnp.max(s, axis=-1, keepdims=True), sink)
                    p = jnp.exp(s - m)
                    denom = jnp.sum(p, axis=-1, keepdims=True) + jnp.exp(sink - m)
                    ps.append(p.astype(BF16))
                    invs.append(1.0 / denom)
                o = _dot(jnp.concatenate(ps, axis=1), v_bd)
                o = o * jnp.where(lane_o, invs[0], invs[1])
                o_ref[rows, c0:c0 + LANES] = o.astype(o_ref.dtype)


def _attn_b(qkv, sinks, seq_len, tq):
    t = qkv.shape[0]
    sub = tq // ATT_BLK
    kv_col = WIDTH_QB // (4 * LANES)
    return pl.pallas_call(
        functools.partial(_attn_b_kernel, tq=tq, chunks_per_seq=seq_len // tq),
        grid=(t // tq,),
        in_specs=[pl.BlockSpec(memory_space=pltpu.SMEM),
                  pl.BlockSpec((tq, WIDTH_QB), lambda c: (c, 0)),
                  pl.BlockSpec((tq, 4 * LANES), lambda c: (c, kv_col)),
                  pl.BlockSpec((ATT_BLK, 4 * LANES), lambda c: (jnp.maximum(c * sub - 1, 0), kv_col))],
        out_specs=pl.BlockSpec((tq, WIDTH_QB), lambda c: (c, 0)),
        out_shape=jax.ShapeDtypeStruct((t, WIDTH_QB), BF16),
        compiler_params=_cparams(("arbitrary",)),
        name="attn_b",
    )(sinks, qkv, qkv, qkv)


def _attn_c_kernel(q_ref, mkv_ref, o_ref, *, tq, sub):
    for h in range(N_HEADS_C):
        hs = slice(h * HEAD_DIM_C, (h + 1) * HEAD_DIM_C)
        mk = mkv_ref[:, hs]
        mv = mkv_ref[:, WIDTH_C + h * HEAD_DIM_C:WIDTH_C + (h + 1) * HEAD_DIM_C]
        for i in range(tq // sub):
            rows = slice(i * sub, (i + 1) * sub)
            s = _dot_nt(q_ref[rows, hs], mk)
            _, p, l = _softmax_rows(s)
            o_ref[rows, hs] = (_dot(p.astype(BF16), mv) / l).astype(o_ref.dtype)


def _attn_c(qc, mkv, seq_len, n_mem, tq):
    t = qc.shape[0]
    per_seq = seq_len // tq
    return pl.pallas_call(
        functools.partial(_attn_c_kernel, tq=tq, sub=min(tq, 256)),
        grid=(t // tq,),
        in_specs=[pl.BlockSpec((tq, WIDTH_C), lambda c: (c, 0)),
                  pl.BlockSpec((n_mem, 2 * WIDTH_C), lambda c: (c // per_seq, 0))],
        out_specs=pl.BlockSpec((tq, WIDTH_C), lambda c: (c, 0)),
        out_shape=jax.ShapeDtypeStruct((t, WIDTH_C), BF16),
        compiler_params=_cparams(("arbitrary",)),
        name="attn_c",
    )(qc, mkv)


def _slab_store(ref, val):
    rows, width = val.shape
    n = width // LANES
    for s in range(n):
        ref[pl.ds(s, rows, stride=n), :] = val[:, s * LANES:(s + 1) * LANES]


def _slab_load(ref, rows, width):
    n = width // LANES
    return jnp.concatenate([ref[pl.ds(s, rows, stride=n), :] for s in range(n)], axis=1)


def _layer_norm(z, g, b):
    mu = jnp.mean(z, axis=-1, keepdims=True)
    zc = z - mu
    var = jnp.mean(zc * zc, axis=-1, keepdims=True)
    return zc * lax.rsqrt(var + LN_EPS) * g + b


def _merge_kernel(*refs, d_model):
    n_grp = len(DILATIONS_A)
    oa_refs, lse_refs = refs[:n_grp], refs[n_grp:2 * n_grp]
    (yb_ref, yc_ref, gate_ref, x_ref, wa_ref, wb_ref, wc_ref, wo_ref, g_ref, b_ref,
     h_ref, hs_ref, oa_s, lse_s) = refs[2 * n_grp:]
    tm = x_ref.shape[0]

    def natural(ref, stage, g):
        d = DILATIONS_A[g]
        if d == 1:
            return ref[0, 0].astype(F32)
        chunks = GROUP_WIDTH_A // LANES
        for r in range(d):
            for c in range(chunks):
                stage[g, c, pl.ds(r, tm // d, stride=d), :] = ref[0, r, :, c * LANES:(c + 1) * LANES].astype(F32)
        return jnp.concatenate([stage[g, c] for c in range(chunks)], axis=1)

    lses = [natural(lse_refs[g], lse_s, g) for g in range(n_grp)]
    top = functools.reduce(jnp.maximum, lses)
    es = [jnp.exp(l - top) for l in lses]
    tot = functools.reduce(jnp.add, es)
    ya = functools.reduce(jnp.add, [e * natural(oa_refs[g], oa_s, g) for g, e in enumerate(es)]) / tot
    gate = lambda g: gate_ref[:, g * d_model:(g + 1) * d_model].astype(F32)
    merged = gate(0) * _dot(ya.astype(BF16), wa_ref[...])
    merged += gate(1) * _dot(yb_ref[...], wb_ref[...])
    merged += gate(2) * _dot(yc_ref[...], wc_ref[...])
    mix = _dot(merged.astype(BF16), wo_ref[...])
    h = _layer_norm(ALPHA * x_ref[...] + mix, g_ref[...], b_ref[...])
    h_ref[...] = h
    _slab_store(hs_ref, h)


def _merge(oas, lses, yb, yc, gates, x2, wa, wb, wc, wo, ln_g, ln_b, seq_len, tm):
    t, d_model = x2.shape
    per_seq = seq_len // tm
    row = lambda w: pl.BlockSpec((tm, w), lambda i: (i, 0))
    grp = lambda d: pl.BlockSpec((1, d, tm // d, GROUP_WIDTH_A), lambda i: (i // per_seq, 0, i % per_seq, 0))
    const = lambda a: pl.BlockSpec(a.shape, lambda i: (0, 0), pipeline_mode=pl.Buffered(1))
    n_grp = len(DILATIONS_A)
    return pl.pallas_call(
        functools.partial(_merge_kernel, d_model=d_model),
        grid=(t // tm,),
        in_specs=[grp(d) for d in DILATIONS_A] * 2 + [row(WIDTH_QB), row(WIDTH_C),
                  row(3 * d_model), row(d_model), const(wa), const(wb), const(wc), const(wo),
                  const(ln_g), const(ln_b)],
        out_specs=[row(d_model), pl.BlockSpec((tm * (d_model // LANES), LANES), lambda i: (i, 0))],
        out_shape=[jax.ShapeDtypeStruct((t, d_model), F32),
                   jax.ShapeDtypeStruct((t * (d_model // LANES), LANES), F32)],
        scratch_shapes=[pltpu.VMEM((n_grp, GROUP_WIDTH_A // LANES, tm, LANES), F32)] * 2,
        compiler_params=_cparams(("arbitrary",)),
        name="merge",
    )(*oas, *lses, yb, yc, gates, x2, wa, wb, wc, wo, ln_g, ln_b)


def _router_kernel(h_ref, whi_ref, wlo_ref, bias_ref, idx_ref, wgt_ref, cnt_ref, *, n_experts):
    h = h_ref[...]
    h_hi = h.astype(BF16)
    h_lo = (h - h_hi.astype(F32)).astype(BF16)
    logits = _dot_nt(whi_ref[...], h_hi) + _dot_nt(whi_ref[...], h_lo) + _dot_nt(wlo_ref[...], h_hi)
    scores = jax.nn.sigmoid(logits)
    biased = scores + bias_ref[...]
    tm = scores.shape[1]
    per_group = n_experts // N_EXPERT_GROUPS
    neg_inf = -jnp.inf

    b3 = biased.reshape(N_EXPERT_GROUPS, per_group, tm)
    i3 = lax.broadcasted_iota(jnp.int32, b3.shape, 1)
    m1 = jnp.max(b3, axis=1, keepdims=True)
    a1 = jnp.min(jnp.where(b3 == m1, i3, per_group), axis=1, keepdims=True)
    m2 = jnp.max(jnp.where(i3 == a1, neg_inf, b3), axis=1, keepdims=True)
    gscore = (m1 + m2).reshape(N_EXPERT_GROUPS, tm)

    gi = lax.broadcasted_iota(jnp.int32, gscore.shape, 0)
    chosen = jnp.zeros(gscore.shape, jnp.int32)
    for _ in range(TOPK_EXPERT_GROUPS):
        gm = jnp.max(gscore, axis=0, keepdims=True)
        ga = jnp.min(jnp.where(gscore == gm, gi, N_EXPERT_GROUPS), axis=0, keepdims=True)
        hit = gi == ga
        chosen = jnp.where(hit, 1, chosen)
        gscore = jnp.where(hit, neg_inf, gscore)

    cur = jnp.where(chosen.reshape(N_EXPERT_GROUPS, 1, tm) > 0, b3, neg_inf).reshape(n_experts, tm)
    ei = lax.broadcasted_iota(jnp.int32, cur.shape, 0)
    idxs, wgts = [], []
    member = jnp.zeros(cur.shape, F32)
    for _ in range(TOP_K):
        m = jnp.max(cur, axis=0, keepdims=True)
        a = jnp.min(jnp.where(cur == m, ei, n_experts), axis=0, keepdims=True)
        hit = ei == a
        idxs.append(a)
        wgts.append(jnp.sum(jnp.where(hit, scores, 0.0), axis=0, keepdims=True))
        member = jnp.where(hit, 1.0, member)
        cur = jnp.where(hit, neg_inf, cur)
    wsum = functools.reduce(jnp.add, wgts)
    idx_ref[...] = jnp.concatenate(idxs, axis=0)
    wgt_ref[...] = jnp.concatenate(wgts, axis=0) / wsum * ROUTED_SCALE

    @pl.when(pl.program_id(0) == 0)
    def _():
        cnt_ref[...] = jnp.zeros_like(cnt_ref)

    cnt_ref[...] += functools.reduce(
        jnp.add, [member[:, c * LANES:(c + 1) * LANES] for c in range(tm // LANES)])


def _router(h1, w_hi, w_lo, bias, tm):
    t, d_model = h1.shape
    n_experts = w_hi.shape[0]
    const = lambda a: pl.BlockSpec(a.shape, lambda i: (0, 0))
    return pl.pallas_call(
        functools.partial(_router_kernel, n_experts=n_experts),
        grid=(t // tm,),
        in_specs=[pl.BlockSpec((tm, d_model), lambda i: (i, 0)), const(w_hi), const(w_lo), const(bias)],
        out_specs=[pl.BlockSpec((TOP_K, tm), lambda i: (0, i)),
                   pl.BlockSpec((TOP_K, tm), lambda i: (0, i)),
                   pl.BlockSpec((n_experts, LANES), lambda i: (0, 0))],
        out_shape=[jax.ShapeDtypeStruct((TOP_K, t), jnp.int32), jax.ShapeDtypeStruct((TOP_K, t), F32),
                   jax.ShapeDtypeStruct((n_experts, LANES), F32)],
        compiler_params=_cparams(("arbitrary",)),
        name="router",
    )(h1, w_hi, w_lo, bias)


def _rank_kernel(idx_ref, base_ref, dest_ref, carry, *, n_experts):
    @pl.when(pl.program_id(0) == 0)
    def _():
        carry[...] = jnp.zeros_like(carry)

    idx = idx_ref[...]
    tm = idx.shape[1]
    ei = lax.broadcasted_iota(jnp.int32, (n_experts, tm), 0)
    hits = [ei == idx[k:k + 1, :] for k in range(TOP_K)]
    member = functools.reduce(jnp.add, [jnp.where(h, 1.0, 0.0) for h in hits])
    earlier = (lax.broadcasted_iota(jnp.int32, (tm, tm), 0)
               < lax.broadcasted_iota(jnp.int32, (tm, tm), 1))
    before = _dot(member.astype(BF16), jnp.where(earlier, 1.0, 0.0).astype(BF16))
    row = base_ref[...] + carry[...] + before
    dest = [jnp.sum(jnp.where(h, row, 0.0), axis=0, keepdims=True) for h in hits]
    dest_ref[...] = jnp.concatenate(dest, axis=0).astype(jnp.int32)
    carry[...] += jnp.sum(member, axis=1, keepdims=True)


def _ranks(top_idx, base, tm):
    k, t = top_idx.shape
    n_experts = base.shape[0]
    return pl.pallas_call(
        functools.partial(_rank_kernel, n_experts=n_experts),
        grid=(t // tm,),
        in_specs=[pl.BlockSpec((k, tm), lambda i: (0, i)), pl.BlockSpec((n_experts, 1), lambda i: (0, 0))],
        out_specs=pl.BlockSpec((k, tm), lambda i: (0, i)),
        out_shape=jax.ShapeDtypeStruct((k, t), jnp.int32),
        scratch_shapes=[pltpu.VMEM((n_experts, 1), F32)],
        compiler_params=_cparams(("arbitrary",)),
        name="ranks",
    )(top_idx, base)


def _expert_kernel(blk0_ref, nblk_ref, tab_hbm, h_hbm, wg_ref, wu_ref, wd_ref, out_hbm,
                   tab, wgu_s, wd_s, xb_s, xbuf, ybuf, gsem, ssem, tsem, *, n_tok, d_expert):
    n_real_rows = TOP_K * n_tok
    d_model = xb_s.shape[1]
    sl = d_model // LANES
    e = pl.program_id(0)
    nb = nblk_ref[e]
    b0 = blk0_ref[e]

    def tab_copy(row):
        slot = row & (TAB_RING - 1)
        return pltpu.make_async_copy(tab_hbm.at[pl.ds(row, 1), :], tab.at[pl.ds(slot, 1), :], tsem.at[slot])

    def slab(ref, row):
        return ref.at[pl.ds(pl.multiple_of(row * sl, sl), sl), :]

    def start_gather(g, buf):
        trow = (g + 1) & (TAB_RING - 1)
        for r in range(MOE_ROWS):
            tok = tab[trow, r] & (n_tok - 1)
            pltpu.make_async_copy(slab(h_hbm, tok), slab(xbuf.at[buf], r), gsem.at[buf]).start(priority=r % 2)

    def start_scatter(g, buf):
        trow = (g + 1) & (TAB_RING - 1)
        for r in range(MOE_ROWS):
            pltpu.make_async_copy(slab(ybuf.at[buf], r), slab(out_hbm, tab[trow, r]),
                                  ssem.at[buf]).start(priority=r % 2)

    def wait_gather(buf):
        pltpu.make_async_copy(h_hbm.at[pl.ds(0, MOE_ROWS * sl), :], xbuf.at[buf], gsem.at[buf]).wait()

    def wait_scatter(buf):
        pltpu.make_async_copy(ybuf.at[buf], out_hbm.at[pl.ds(0, MOE_ROWS * sl), :], ssem.at[buf]).wait()

    @pl.when(e == 0)
    def _():
        for row in range(3):
            tab_copy(row).start()
        ybuf[...] = jnp.zeros_like(ybuf)
        tab_copy(0).wait()
        tab_copy(1).wait()

        for r in range(MOE_ROWS):
            pltpu.make_async_copy(slab(ybuf.at[0], r), slab(out_hbm, n_real_rows + r),
                                  ssem.at[0]).start(priority=r % 2)
        start_gather(0, 0)

    @pl.when(nb > 0)
    def _():
        wgu_s[:, :d_expert] = wg_ref[0].astype(BF16)
        wgu_s[:, d_expert:] = wu_ref[0].astype(BF16)
        wd_s[...] = wd_ref[0].astype(BF16)

        def block(b, _):
            g = b0 + b
            buf = g & 1
            tab_copy(g + 3).start()
            tab_copy(g + 2).wait()
            wait_gather(buf)
            wait_scatter(buf)
            xb_s[...] = _slab_load(xbuf.at[buf], MOE_ROWS, d_model).astype(BF16)
            start_gather(g + 1, 1 - buf)
            start_scatter(g - 1, 1 - buf)
            gu = _dot(xb_s[...], wgu_s[...])
            act = gu[:, :d_expert]
            hid = (act * jax.nn.sigmoid(act) * gu[:, d_expert:]).astype(BF16)
            _slab_store(ybuf.at[buf], _dot(hid, wd_s[...]))
            return 0

        lax.fori_loop(0, nb, block, 0)

    @pl.when(e == pl.num_programs(0) - 1)
    def _():
        n_act = b0 + nb
        last = (n_act - 1) & 1
        start_scatter(n_act - 1, last)
        tab_copy(n_act + 2).wait()
        wait_scatter(1 - last)
        wait_scatter(last)
        wait_gather(1 - last)


def _experts(blk0, nblk, dest, h1_slab, w_gate, w_up, w_down):
    n_experts, d_model, d_expert = w_gate.shape
    sl = d_model // LANES
    n_tok = h1_slab.shape[0] // sl
    n_assign = TOP_K * n_tok
    n_blocks = n_assign // MOE_ROWS + n_experts + 2
    assert n_tok & (n_tok - 1) == 0
    rows = jnp.arange((n_blocks + 1) * MOE_ROWS, dtype=jnp.int32)
    pad = n_assign + (((rows // MOE_ROWS) + 1) % 2) * MOE_ROWS + rows % MOE_ROWS
    tab = pad.at[dest.reshape(n_assign) + MOE_ROWS].set(jnp.arange(n_assign, dtype=jnp.int32),
                                                        unique_indices=True)
    return pl.pallas_call(
        functools.partial(_expert_kernel, n_tok=n_tok, d_expert=d_expert),
        grid_spec=pltpu.PrefetchScalarGridSpec(
            num_scalar_prefetch=2,
            grid=(n_experts,),
            in_specs=[pl.BlockSpec(memory_space=pl.ANY),
                      pl.BlockSpec(memory_space=pl.ANY),
                      pl.BlockSpec((1, d_model, d_expert), lambda e, *_: (e, 0, 0)),
                      pl.BlockSpec((1, d_model, d_expert), lambda e, *_: (e, 0, 0)),
                      pl.BlockSpec((1, d_expert, d_model), lambda e, *_: (e, 0, 0))],
            out_specs=pl.BlockSpec(memory_space=pl.ANY),
            scratch_shapes=[pltpu.SMEM((TAB_RING, MOE_ROWS), jnp.int32),
                            pltpu.VMEM((d_model, 2 * d_expert), BF16),
                            pltpu.VMEM((d_expert, d_model), BF16),
                            pltpu.VMEM((MOE_ROWS, d_model), BF16),
                            pltpu.VMEM((2, MOE_ROWS * sl, LANES), F32),
                            pltpu.VMEM((2, MOE_ROWS * sl, LANES), F32),
                            pltpu.SemaphoreType.DMA((2,)),
                            pltpu.SemaphoreType.DMA((2,)),
                            pltpu.SemaphoreType.DMA((TAB_RING,))],
        ),
        out_shape=jax.ShapeDtypeStruct(((n_assign + 2 * MOE_ROWS) * sl, LANES), F32),
        compiler_params=_cparams(("arbitrary",)),
        name="experts",
    )(blk0, nblk, tab.reshape(n_blocks + 1, MOE_ROWS), h1_slab, w_gate, w_up, w_down)


def _final_kernel(*refs, d_expert):
    h_ref, wt_ref = refs[0], refs[1]
    routed_refs = refs[2:2 + TOP_K]
    wgu_ref, wd_ref, g_ref, b_ref, o_ref = refs[2 + TOP_K:]
    h = h_ref[...]
    gu = _dot(h.astype(BF16), wgu_ref[...])
    act = gu[:, :d_expert]
    hid = (act * jax.nn.sigmoid(act) * gu[:, d_expert:]).astype(BF16)
    ff = _dot(hid, wd_ref[...])
    wt = wt_ref[...]
    rows, d_model = h.shape
    for k in range(TOP_K):
        ff += _slab_load(routed_refs[k], rows, d_model) * wt[:, k:k + 1]
    o_ref[...] = _layer_norm(ALPHA * h + ff, g_ref[...], b_ref[...])


def _final(h1, wt, routed, w_gu, w_d, ln_g, ln_b, tm):
    t, d_model = h1.shape
    d_expert = w_d.shape[0]
    per_k = t // tm
    row = lambda w: pl.BlockSpec((tm, w), lambda i: (i, 0))
    const = lambda a: pl.BlockSpec(a.shape, lambda i: (0, 0))
    routed_specs = [pl.BlockSpec((tm * (d_model // LANES), LANES), lambda i, k=k: (k * per_k + i, 0))
                    for k in range(TOP_K)]
    return pl.pallas_call(
        functools.partial(_final_kernel, d_expert=d_expert),
        grid=(t // tm,),
        in_specs=[row(d_model), row(TOP_K)] + routed_specs + [const(w_gu), const(w_d), const(ln_g), const(ln_b)],
        out_specs=row(d_model),
        out_shape=jax.ShapeDtypeStruct((t, d_model), F32),
        compiler_params=_cparams(("arbitrary",)),
        name="final",
    )(h1, wt, *([routed] * TOP_K), w_gu, w_d, ln_g, ln_b)


def _layer(h, mem, w_in, w_mem_kv, sinks, w_a, w_b, w_c, w_out, ln1_g, ln1_b, router_w, router_bias,
           w_eg, w_eu, w_ed, w_sg, w_su, w_sd, ln2_g, ln2_b):
    bsz, seq_len, d_model = h.shape
    n_tok = bsz * seq_len
    n_mem = mem.shape[1]
    n_experts = router_w.shape[1]
    x2 = h.reshape(n_tok, d_model)
    xb = x2.astype(BF16)

    o_qa, o_ka, o_va = 0, WIDTH_A, 2 * WIDTH_A
    o_qb = 3 * WIDTH_A
    o_kb = o_qb + WIDTH_QB
    o_vb = o_kb + WIDTH_KVB
    o_qc = o_vb + WIDTH_KVB
    o_gate = o_qc + WIDTH_C
    wb16 = w_in.astype(BF16)
    cols = lambda o, n: wb16[:, o:o + n]
    chunks_a = WIDTH_A // LANES
    scale_a = HEAD_DIM_A ** -0.5
    tabs_a = _rope_tables(seq_len, HEAD_DIM_A)
    qkv_a = _proj(xb, cols(0, 3 * WIDTH_A),
                  ((("rope128", scale_a),) * chunks_a, (("rope128", 1.0),) * chunks_a,
                   (("plain", 1.0),) * chunks_a),
                  WIDTH_A, BF16, tabs_a, seq_len, dilations=DILATIONS_A)

    hb = HEAD_DIM_B
    dup = lambda o: [cols(o + n * hb, hb) for n in range(N_KV_HEADS_B) for _ in range(2)]
    w_qkv_b = jnp.concatenate([cols(o_qb, WIDTH_QB)] + dup(o_kb) + dup(o_vb), axis=1)
    q_chunks = WIDTH_QB // LANES
    modes_b = ((("rope64", HEAD_DIM_B ** -0.5),) * q_chunks + (("rope64", 1.0),) * N_KV_HEADS_B
               + (("plain", 1.0),) * N_KV_HEADS_B,)
    qkv_b = _proj(xb, w_qkv_b, modes_b, w_qkv_b.shape[1], BF16, _rope_tables(seq_len, HEAD_DIM_B), seq_len)

    q_c = _proj(xb, cols(o_qc, WIDTH_C), ((("plain", HEAD_DIM_C ** -0.5),) * (WIDTH_C // LANES),),
                WIDTH_C, BF16)
    gate_tn = 1536
    gates = _proj(xb, cols(o_gate, 3 * d_model),
                  ((("sigmoid", 1.0),) * (gate_tn // LANES),) * (3 * d_model // gate_tn), gate_tn, BF16)
    mkv = _proj(mem.reshape(bsz * n_mem, d_model).astype(BF16), w_mem_kv.astype(BF16),
                ((("plain", 1.0),) * (2 * WIDTH_C // LANES),), 2 * WIDTH_C, BF16)

    tq = min(512, seq_len // max(DILATIONS_A))
    oas, lses = [], []
    for d, qkv_g in zip(DILATIONS_A, qkv_a):
        first = (jnp.arange(n_tok // tq, dtype=jnp.int32) % (seq_len // d // tq) == 0).astype(jnp.int32)
        o_g, lse_g = _attn_a(qkv_g.reshape(n_tok, 3 * GROUP_WIDTH_A), first, tq)
        oas.append(o_g.reshape(bsz, d, seq_len // d, GROUP_WIDTH_A))
        lses.append(lse_g.reshape(bsz, d, seq_len // d, GROUP_WIDTH_A))

    yb = _attn_b(qkv_b, sinks.astype(F32), seq_len, tq)
    yc = _attn_c(q_c, mkv, seq_len, n_mem, tq)

    h1, h1_slab = _merge(oas, lses, yb, yc, gates, x2, w_a.astype(BF16), w_b.astype(BF16), w_c.astype(BF16),
                         w_out.astype(BF16), ln1_g.reshape(1, d_model), ln1_b.reshape(1, d_model), seq_len,
                         tm=256)

    rw_t = router_w.T
    rw_hi = rw_t.astype(BF16)
    rw_lo = (rw_t - rw_hi.astype(F32)).astype(BF16)
    top_idx, top_w, cnt = _router(h1, rw_hi, rw_lo, router_bias.reshape(n_experts, 1).astype(F32), tm=256)
    counts = jnp.sum(cnt, axis=1).astype(jnp.int32)
    nblk = (counts + MOE_ROWS - 1) // MOE_ROWS
    blk0 = jnp.cumsum(nblk) - nblk
    dest = _ranks(top_idx, (blk0 * MOE_ROWS).astype(F32).reshape(n_experts, 1), tm=256)
    routed = _experts(blk0, nblk, dest, h1_slab, w_eg, w_eu, w_ed)
    w_sgu = jnp.concatenate([w_sg, w_su], axis=1).astype(BF16)
    out = _final(h1, top_w.T, routed, w_sgu, w_sd.astype(BF16), ln2_g.reshape(1, d_model),
                 ln2_b.reshape(1, d_model), tm=256)
    return out.reshape(bsz, seq_len, d_model)


def kernel(x, mem, w_in, w_mem_kv, attn_sinks, w_branch_a, w_branch_b, w_branch_c, w_out, ln1_g, ln1_b,
           router_w, router_bias, w_exp_gate, w_exp_up, w_exp_down, w_sh_gate, w_sh_up, w_sh_down,
           ln2_g, ln2_b):
    h = x
    for layer in range(DEPTH):
        h = _layer(h, mem, w_in[layer], w_mem_kv[layer], attn_sinks[layer], w_branch_a[layer],
                   w_branch_b[layer], w_branch_c[layer], w_out[layer], ln1_g[layer], ln1_b[layer],
                   router_w[layer], router_bias[layer], w_exp_gate[layer], w_exp_up[layer],
                   w_exp_down[layer], w_sh_gate[layer], w_sh_up[layer], w_sh_down[layer],
                   ln2_g[layer], ln2_b[layer])
    return h
```

```python
import functools

import jax
import jax.numpy as jnp
from jax import lax
from jax.experimental import pallas as pl
from jax.experimental.pallas import tpu as pltpu

F32 = jnp.float32
BF16 = jnp.bfloat16

DEPTH = 1
HEAD_DIM_A = 128
DILATIONS_A = (1, 4, 16)
N_BACK_A = 128
HEADS_PER_GROUP_A = 4
GROUP_WIDTH_A = HEADS_PER_GROUP_A * HEAD_DIM_A
WIDTH_A = len(DILATIONS_A) * GROUP_WIDTH_A
HEAD_DIM_B = 64
N_Q_HEADS_B = 16
N_KV_HEADS_B = 2
WIDTH_QB = N_Q_HEADS_B * HEAD_DIM_B
WIDTH_KVB = N_KV_HEADS_B * HEAD_DIM_B
N_BACK_B = 127
N_HEADS_C = 4
HEAD_DIM_C = 256
WIDTH_C = N_HEADS_C * HEAD_DIM_C
ROPE_THETA = 10000.0
TOP_K = 8
N_EXPERT_GROUPS = 8
TOPK_EXPERT_GROUPS = 4
ROUTED_SCALE = 2.5
ALPHA = (2 * DEPTH) ** 0.25
LN_EPS = 1e-5

LANES = 128
ATT_BLK = 128
MOE_ROWS = 128
MOE_BUFS = 4
MOE_AHEAD = 2
TAB_RING = 8
VMEM_LIMIT = 56 * 1024 * 1024
NEG = -1e30


def _cparams(sem):
    return pltpu.CompilerParams(dimension_semantics=sem, vmem_limit_bytes=VMEM_LIMIT)


def _dot(a, b):
    return jnp.dot(a, b, preferred_element_type=F32)


def _dot_nt(a, b):
    return lax.dot_general(a, b, (((1,), (1,)), ((), ())), preferred_element_type=F32)


def _rope_chunk(a, cos, sin, head_dim):
    if head_dim == 128:
        partner = pltpu.roll(a, 64, 1)
    else:
        lane = lax.broadcasted_iota(jnp.int32, a.shape, 1)
        partner = jnp.where((lane % 64) < 32, pltpu.roll(a, 96, 1), pltpu.roll(a, 32, 1))
    return a * cos + partner * sin


def _proj_kernel(*refs, modes, has_tables, dilations):
    n_out = 1 if dilations is None else len(dilations)
    x_ref, w_ref = refs[:2]
    cos_ref, sin_ref = refs[2:4] if has_tables else (None, None)
    o_refs = refs[2 + 2 * has_tables:2 + 2 * has_tables + n_out]
    stage = refs[-1] if dilations is not None else None
    j = pl.program_id(0)
    acc = _dot(x_ref[...], w_ref[...])
    tm = acc.shape[0]

    def store(c, a):
        if dilations is None:
            o_refs[0][:, c * LANES:(c + 1) * LANES] = a.astype(o_refs[0].dtype)
            return
        per_group = len(modes[0]) // n_out
        gi, cg = divmod(c, per_group)
        d, o_ref = dilations[gi], o_refs[gi]
        lanes = slice(cg * LANES, (cg + 1) * LANES)
        if d == 1:
            o_ref[0, 0, :, lanes] = a.astype(o_ref.dtype)
        else:
            stage[c] = a
            for r in range(d):
                o_ref[0, r, :, lanes] = stage[c, pl.ds(r, tm // d, stride=d), :].astype(o_ref.dtype)

    def epilogue(chunk_modes):
        for c, (kind, scale) in enumerate(chunk_modes):
            a = acc[:, c * LANES:(c + 1) * LANES]
            if kind == "rope128":
                a = _rope_chunk(a, cos_ref[...], sin_ref[...], 128)
            elif kind == "rope64":
                a = _rope_chunk(a, cos_ref[...], sin_ref[...], 64)
            elif kind == "sigmoid":
                a = jax.nn.sigmoid(a)
            if scale != 1.0:
                a = a * scale
            store(c, a)

    if all(m == modes[0] for m in modes):
        epilogue(modes[0])
    else:
        for jj, chunk_modes in enumerate(modes):
            pl.when(j == jj)(functools.partial(epilogue, chunk_modes))


def _proj(x, w, modes, tn, out_dtype, tables=None, seq_len=None, tm=1024, dilations=None):
    m, k = x.shape
    n = w.shape[1]
    tm = min(tm, m)
    assert m % tm == 0 and n % tn == 0 and len(modes) == n // tn
    in_specs = [pl.BlockSpec((tm, k), lambda j, i: (i, 0)),
                pl.BlockSpec((k, tn), lambda j, i: (0, j))]
    args = [x, w]
    if tables is not None:
        per_seq = seq_len // tm
        tab_spec = pl.BlockSpec((tm, LANES), lambda j, i: (i % per_seq, 0))
        in_specs += [tab_spec, tab_spec]
        args += list(tables)
    scratch = []
    if dilations is None:
        out_specs = pl.BlockSpec((tm, tn), lambda j, i: (i, j))
        out_shape = jax.ShapeDtypeStruct((m, n), out_dtype)
    else:
        per_seq = seq_len // tm
        gw = tn // len(dilations)
        out_specs = [pl.BlockSpec((1, d, tm // d, gw), lambda j, i: (i // per_seq, 0, i % per_seq, j))
                     for d in dilations]
        out_shape = [jax.ShapeDtypeStruct((m // seq_len, d, seq_len // d, (n // tn) * gw), out_dtype)
                     for d in dilations]
        scratch = [pltpu.VMEM((tn // LANES, tm, LANES), F32)]
    return pl.pallas_call(
        functools.partial(_proj_kernel, modes=modes, has_tables=tables is not None, dilations=dilations),
        grid=(n // tn, m // tm),
        in_specs=in_specs,
        out_specs=out_specs,
        out_shape=out_shape,
        scratch_shapes=scratch,
        compiler_params=_cparams(("arbitrary", "arbitrary")),
        name="proj",
    )(*args)


def _rope_tables(seq_len, head_dim):
    half = head_dim // 2
    inv_freq = jnp.power(ROPE_THETA, -jnp.arange(half, dtype=F32) * 2.0 / head_dim)
    ang = jnp.arange(seq_len).astype(F32)[:, None] * inv_freq[None, :]
    cos, sin = jnp.cos(ang), jnp.sin(ang)
    reps = LANES // head_dim
    cos_t = jnp.tile(jnp.concatenate([cos, cos], axis=-1), (1, reps))
    sin_t = jnp.tile(jnp.concatenate([-sin, sin], axis=-1), (1, reps))
    return cos_t, sin_t


def _softmax_rows(s):
    m = jnp.max(s, axis=-1, keepdims=True)
    p = jnp.exp(s - m)
    l = jnp.sum(p, axis=-1, keepdims=True)
    return m, p, l


def _attn_a_kernel(first_ref, q_ref, kc_ref, vc_ref, kp_ref, vp_ref, o_ref, lse_ref, *, tq):
    c = pl.program_id(0)
    not_first = first_ref[c] == 0
    row = lax.broadcasted_iota(jnp.int32, (ATT_BLK, 2 * ATT_BLK), 0)
    col = lax.broadcasted_iota(jnp.int32, (ATT_BLK, 2 * ATT_BLK), 1)
    band = (col >= row) & (col <= row + N_BACK_A)
    band_first = band & ((col >= ATT_BLK) | not_first)
    for h in range(HEADS_PER_GROUP_A):
        hs = slice(h * HEAD_DIM_A, (h + 1) * HEAD_DIM_A)
        for i in range(tq // ATT_BLK):
            rows = slice(i * ATT_BLK, (i + 1) * ATT_BLK)
            q = q_ref[rows, hs]
            if i == 0:
                k = jnp.concatenate([kp_ref[:, hs], kc_ref[rows, hs]], axis=0)
                v = jnp.concatenate([vp_ref[:, hs], vc_ref[rows, hs]], axis=0)
                mask = band_first
            else:
                kv_rows = slice((i - 1) * ATT_BLK, (i + 1) * ATT_BLK)
                k = kc_ref[kv_rows, hs]
                v = vc_ref[kv_rows, hs]
                mask = band
            s = jnp.where(mask, _dot_nt(q, k), NEG)
            m, p, l = _softmax_rows(s)
            o = _dot(p.astype(BF16), v) / l
            o_ref[rows, hs] = o.astype(o_ref.dtype)
            lse_ref[rows, hs] = jnp.broadcast_to(m + jnp.log(l), (ATT_BLK, HEAD_DIM_A))


def _attn_a(qkv, first, tq):
    r = qkv.shape[0]
    gw = GROUP_WIDTH_A
    sub = tq // ATT_BLK
    cur = lambda col: pl.BlockSpec((tq, gw), lambda c, f: (c, col))
    prev = lambda col: pl.BlockSpec((ATT_BLK, gw), lambda c, f: (jnp.maximum(c * sub - 1, 0), col))
    return pl.pallas_call(
        functools.partial(_attn_a_kernel, tq=tq),
        grid_spec=pltpu.PrefetchScalarGridSpec(
            num_scalar_prefetch=1,
            grid=(r // tq,),
            in_specs=[cur(0), cur(1), cur(2), prev(1), prev(2)],
            out_specs=[pl.BlockSpec((tq, gw), lambda c, f: (c, 0)),
                       pl.BlockSpec((tq, gw), lambda c, f: (c, 0))],
        ),
        out_shape=[jax.ShapeDtypeStruct((r, gw), BF16), jax.ShapeDtypeStruct((r, gw), F32)],
        compiler_params=_cparams(("arbitrary",)),
        name="attn_a",
    )(first, qkv, qkv, qkv, qkv, qkv)


def _attn_b_kernel(sink_ref, q_ref, kvc_ref, kvp_ref, o_ref, *, tq, chunks_per_seq):
    c = pl.program_id(0)
    not_first = (c % chunks_per_seq) != 0
    row = lax.broadcasted_iota(jnp.int32, (ATT_BLK, 2 * ATT_BLK), 0)
    col = lax.broadcasted_iota(jnp.int32, (ATT_BLK, 2 * ATT_BLK), 1)
    band = (col >= row + 1) & (col <= row + 1 + N_BACK_B)
    band_first = band & ((col >= ATT_BLK) | not_first)
    lane_kv = lax.broadcasted_iota(jnp.int32, (2 * ATT_BLK, LANES), 1) < HEAD_DIM_B
    lane_o = lax.broadcasted_iota(jnp.int32, (ATT_BLK, LANES), 1) < HEAD_DIM_B
    pairs = (N_Q_HEADS_B // N_KV_HEADS_B) // 2
    for i in range(tq // ATT_BLK):
        rows = slice(i * ATT_BLK, (i + 1) * ATT_BLK)
        if i == 0:
            kv = jnp.concatenate([kvp_ref[...], kvc_ref[rows, :]], axis=0)
            mask = band_first
        else:
            kv = kvc_ref[(i - 1) * ATT_BLK:(i + 1) * ATT_BLK, :]
            mask = band
        for n in range(N_KV_HEADS_B):
            kk = kv[:, n * LANES:(n + 1) * LANES]
            vv = kv[:, (N_KV_HEADS_B + n) * LANES:(N_KV_HEADS_B + n + 1) * LANES]
            zero = jnp.zeros_like(kk)
            k_bd = jnp.concatenate([jnp.where(lane_kv, kk, zero), jnp.where(lane_kv, zero, kk)], axis=0)
            v_bd = jnp.concatenate([jnp.where(lane_kv, vv, zero), jnp.where(lane_kv, zero, vv)], axis=0)
            for jp in range(pairs):
                c0 = n * (WIDTH_QB // N_KV_HEADS_B) + jp * LANES
                q = q_ref[rows, c0:c0 + LANES]
                s2 = _dot_nt(q, k_bd)
                ps, invs = [], []
                for hh in range(2):
                    sink = sink_ref[n * 2 * pairs + jp * 2 + hh]
                    s = jnp.where(mask, s2[:, hh * 2 * ATT_BLK:(hh + 1) * 2 * ATT_BLK], NEG)
                    m = jnp.maximum(jnp.max(s, axis=-1, keepdims=True), sink)
                    p = jnp.exp(s - m)
                    denom = jnp.sum(p, axis=-1, keepdims=True) + jnp.exp(sink - m)
                    ps.append(p.astype(BF16))
                    invs.append(1.0 / denom)
                o = _dot(jnp.concatenate(ps, axis=1), v_bd)
                o = o * jnp.where(lane_o, invs[0], invs[1])
                o_ref[rows, c0:c0 + LANES] = o.astype(o_ref.dtype)


def _attn_b(qkv, sinks, seq_len, tq):
    t = qkv.shape[0]
    sub = tq // ATT_BLK
    kv_col = WIDTH_QB // (4 * LANES)
    return pl.pallas_call(
        functools.partial(_attn_b_kernel, tq=tq, chunks_per_seq=seq_len // tq),
        grid=(t // tq,),
        in_specs=[pl.BlockSpec(memory_space=pltpu.SMEM),
                  pl.BlockSpec((tq, WIDTH_QB), lambda c: (c, 0)),
                  pl.BlockSpec((tq, 4 * LANES), lambda c: (c, kv_col)),
                  pl.BlockSpec((ATT_BLK, 4 * LANES), lambda c: (jnp.maximum(c * sub - 1, 0), kv_col))],
        out_specs=pl.BlockSpec((tq, WIDTH_QB), lambda c: (c, 0)),
        out_shape=jax.ShapeDtypeStruct((t, WIDTH_QB), BF16),
        compiler_params=_cparams(("arbitrary",)),
        name="attn_b",
    )(sinks, qkv, qkv, qkv)


def _attn_c_kernel(q_ref, mkv_ref, o_ref, *, tq, sub):
    for h in range(N_HEADS_C):
        hs = slice(h * HEAD_DIM_C, (h + 1) * HEAD_DIM_C)
        mk = mkv_ref[:, hs]
        mv = mkv_ref[:, WIDTH_C + h * HEAD_DIM_C:WIDTH_C + (h + 1) * HEAD_DIM_C]
        for i in range(tq // sub):
            rows = slice(i * sub, (i + 1) * sub)
            s = _dot_nt(q_ref[rows, hs], mk)
            _, p, l = _softmax_rows(s)
            o_ref[rows, hs] = (_dot(p.astype(BF16), mv) / l).astype(o_ref.dtype)


def _attn_c(qc, mkv, seq_len, n_mem, tq):
    t = qc.shape[0]
    per_seq = seq_len // tq
    return pl.pallas_call(
        functools.partial(_attn_c_kernel, tq=tq, sub=min(tq, 256)),
        grid=(t // tq,),
        in_specs=[pl.BlockSpec((tq, WIDTH_C), lambda c: (c, 0)),
                  pl.BlockSpec((n_mem, 2 * WIDTH_C), lambda c: (c // per_seq, 0))],
        out_specs=pl.BlockSpec((tq, WIDTH_C), lambda c: (c, 0)),
        out_shape=jax.ShapeDtypeStruct((t, WIDTH_C), BF16),
        compiler_params=_cparams(("arbitrary",)),
        name="attn_c",
    )(qc, mkv)


def _slab_store(ref, val):
    rows, width = val.shape
    n = width // LANES
    for s in range(n):
        ref[pl.ds(s, rows, stride=n), :] = val[:, s * LANES:(s + 1) * LANES]


def _slab_load(ref, rows, width):
    n = width // LANES
    return jnp.concatenate([ref[pl.ds(s, rows, stride=n), :] for s in range(n)], axis=1)


def _layer_norm(z, g, b):
    mu = jnp.mean(z, axis=-1, keepdims=True)
    zc = z - mu
    var = jnp.mean(zc * zc, axis=-1, keepdims=True)
    return zc * lax.rsqrt(var + LN_EPS) * g + b


def _merge_kernel(*refs, d_model):
    n_grp = len(DILATIONS_A)
    oa_refs, lse_refs = refs[:n_grp], refs[n_grp:2 * n_grp]
    (yb_ref, yc_ref, gate_ref, x_ref, wa_ref, wb_ref, wc_ref, wo_ref, g_ref, b_ref,
     h_ref, hs_ref, oa_s, lse_s) = refs[2 * n_grp:]
    tm = x_ref.shape[0]

    def natural(ref, stage, g):
        d = DILATIONS_A[g]
        if d == 1:
            return ref[0, 0].astype(F32)
        chunks = GROUP_WIDTH_A // LANES
        for r in range(d):
            for c in range(chunks):
                stage[g, c, pl.ds(r, tm // d, stride=d), :] = ref[0, r, :, c * LANES:(c + 1) * LANES].astype(F32)
        return jnp.concatenate([stage[g, c] for c in range(chunks)], axis=1)

    lses = [natural(lse_refs[g], lse_s, g) for g in range(n_grp)]
    top = functools.reduce(jnp.maximum, lses)
    es = [jnp.exp(l - top) for l in lses]
    tot = functools.reduce(jnp.add, es)
    ya = functools.reduce(jnp.add, [e * natural(oa_refs[g], oa_s, g) for g, e in enumerate(es)]) / tot
    gate = lambda g: gate_ref[:, g * d_model:(g + 1) * d_model].astype(F32)
    merged = gate(0) * _dot(ya.astype(BF16), wa_ref[...])
    merged += gate(1) * _dot(yb_ref[...], wb_ref[...])
    merged += gate(2) * _dot(yc_ref[...], wc_ref[...])
    mix = _dot(merged.astype(BF16), wo_ref[...])
    h = _layer_norm(ALPHA * x_ref[...] + mix, g_ref[...], b_ref[...])
    h_ref[...] = h
    _slab_store(hs_ref, h)


def _merge(oas, lses, yb, yc, gates, x2, wa, wb, wc, wo, ln_g, ln_b, seq_len, tm):
    t, d_model = x2.shape
    per_seq = seq_len // tm
    row = lambda w: pl.BlockSpec((tm, w), lambda i: (i, 0))
    grp = lambda d: pl.BlockSpec((1, d, tm // d, GROUP_WIDTH_A), lambda i: (i // per_seq, 0, i % per_seq, 0))
    const = lambda a: pl.BlockSpec(a.shape, lambda i: (0, 0), pipeline_mode=pl.Buffered(1))
    n_grp = len(DILATIONS_A)
    return pl.pallas_call(
        functools.partial(_merge_kernel, d_model=d_model),
        grid=(t // tm,),
        in_specs=[grp(d) for d in DILATIONS_A] * 2 + [row(WIDTH_QB), row(WIDTH_C),
                  row(3 * d_model), row(d_model), const(wa), const(wb), const(wc), const(wo),
                  const(ln_g), const(ln_b)],
        out_specs=[row(d_model), pl.BlockSpec((tm * (d_model // LANES), LANES), lambda i: (i, 0))],
        out_shape=[jax.ShapeDtypeStruct((t, d_model), F32),
                   jax.ShapeDtypeStruct((t * (d_model // LANES), LANES), F32)],
        scratch_shapes=[pltpu.VMEM((n_grp, GROUP_WIDTH_A // LANES, tm, LANES), F32)] * 2,
        compiler_params=_cparams(("arbitrary",)),
        name="merge",
    )(*oas, *lses, yb, yc, gates, x2, wa, wb, wc, wo, ln_g, ln_b)


def _router_kernel(h_ref, whi_ref, wlo_ref, bias_ref, idx_ref, wgt_ref, cnt_ref, *, n_experts):
    h = h_ref[...]
    h_hi = h.astype(BF16)
    h_lo = (h - h_hi.astype(F32)).astype(BF16)
    logits = _dot_nt(whi_ref[...], h_hi) + _dot_nt(whi_ref[...], h_lo) + _dot_nt(wlo_ref[...], h_hi)
    scores = jax.nn.sigmoid(logits)
    biased = scores + bias_ref[...]
    tm = scores.shape[1]
    per_group = n_experts // N_EXPERT_GROUPS
    neg_inf = -jnp.inf

    b3 = biased.reshape(N_EXPERT_GROUPS, per_group, tm)
    i3 = lax.broadcasted_iota(jnp.int32, b3.shape, 1)
    m1 = jnp.max(b3, axis=1, keepdims=True)
    a1 = jnp.min(jnp.where(b3 == m1, i3, per_group), axis=1, keepdims=True)
    m2 = jnp.max(jnp.where(i3 == a1, neg_inf, b3), axis=1, keepdims=True)
    gscore = (m1 + m2).reshape(N_EXPERT_GROUPS, tm)

    gi = lax.broadcasted_iota(jnp.int32, gscore.shape, 0)
    chosen = jnp.zeros(gscore.shape, jnp.int32)
    for _ in range(TOPK_EXPERT_GROUPS):
        gm = jnp.max(gscore, axis=0, keepdims=True)
        ga = jnp.min(jnp.where(gscore == gm, gi, N_EXPERT_GROUPS), axis=0, keepdims=True)
        hit = gi == ga
        chosen = jnp.where(hit, 1, chosen)
        gscore = jnp.where(hit, neg_inf, gscore)

    cur = jnp.where(chosen.reshape(N_EXPERT_GROUPS, 1, tm) > 0, b3, neg_inf).reshape(n_experts, tm)
    ei = lax.broadcasted_iota(jnp.int32, cur.shape, 0)
    idxs, wgts = [], []
    member = jnp.zeros(cur.shape, F32)
    for _ in range(TOP_K):
        m = jnp.max(cur, axis=0, keepdims=True)
        a = jnp.min(jnp.where(cur == m, ei, n_experts), axis=0, keepdims=True)
        hit = ei == a
        idxs.append(a)
        wgts.append(jnp.sum(jnp.where(hit, scores, 0.0), axis=0, keepdims=True))
        member = jnp.where(hit, 1.0, member)
        cur = jnp.where(hit, neg_inf, cur)
    wsum = functools.reduce(jnp.add, wgts)
    idx_ref[...] = jnp.concatenate(idxs, axis=0)
    wgt_ref[...] = jnp.concatenate(wgts, axis=0) / wsum * ROUTED_SCALE

    @pl.when(pl.program_id(0) == 0)
    def _():
        cnt_ref[...] = jnp.zeros_like(cnt_ref)

    cnt_ref[...] += functools.reduce(
        jnp.add, [member[:, c * LANES:(c + 1) * LANES] for c in range(tm // LANES)])


def _router(h1, w_hi, w_lo, bias, tm):
    t, d_model = h1.shape
    n_experts = w_hi.shape[0]
    const = lambda a: pl.BlockSpec(a.shape, lambda i: (0, 0))
    return pl.pallas_call(
        functools.partial(_router_kernel, n_experts=n_experts),
        grid=(t // tm,),
        in_specs=[pl.BlockSpec((tm, d_model), lambda i: (i, 0)), const(w_hi), const(w_lo), const(bias)],
        out_specs=[pl.BlockSpec((TOP_K, tm), lambda i: (0, i)),
                   pl.BlockSpec((TOP_K, tm), lambda i: (0, i)),
                   pl.BlockSpec((n_experts, LANES), lambda i: (0, 0))],
        out_shape=[jax.ShapeDtypeStruct((TOP_K, t), jnp.int32), jax.ShapeDtypeStruct((TOP_K, t), F32),
                   jax.ShapeDtypeStruct((n_experts, LANES), F32)],
        compiler_params=_cparams(("arbitrary",)),
        name="router",
    )(h1, w_hi, w_lo, bias)


def _rank_kernel(idx_ref, base_ref, dest_ref, carry, *, n_experts):
    @pl.when(pl.program_id(0) == 0)
    def _():
        carry[...] = jnp.zeros_like(carry)

    idx = idx_ref[...]
    tm = idx.shape[1]
    ei = lax.broadcasted_iota(jnp.int32, (n_experts, tm), 0)
    hits = [ei == idx[k:k + 1, :] for k in range(TOP_K)]
    member = functools.reduce(jnp.add, [jnp.where(h, 1.0, 0.0) for h in hits])
    earlier = (lax.broadcasted_iota(jnp.int32, (tm, tm), 0)
               < lax.broadcasted_iota(jnp.int32, (tm, tm), 1))
    before = _dot(member.astype(BF16), jnp.where(earlier, 1.0, 0.0).astype(BF16))
    row = base_ref[...] + carry[...] + before
    dest = [jnp.sum(jnp.where(h, row, 0.0), axis=0, keepdims=True) for h in hits]
    dest_ref[...] = jnp.concatenate(dest, axis=0).astype(jnp.int32)
    carry[...] += jnp.sum(member, axis=1, keepdims=True)


def _ranks(top_idx, base, tm):
    k, t = top_idx.shape
    n_experts = base.shape[0]
    return pl.pallas_call(
        functools.partial(_rank_kernel, n_experts=n_experts),
        grid=(t // tm,),
        in_specs=[pl.BlockSpec((k, tm), lambda i: (0, i)), pl.BlockSpec((n_experts, 1), lambda i: (0, 0))],
        out_specs=pl.BlockSpec((k, tm), lambda i: (0, i)),
        out_shape=jax.ShapeDtypeStruct((k, t), jnp.int32),
        scratch_shapes=[pltpu.VMEM((n_experts, 1), F32)],
        compiler_params=_cparams(("arbitrary",)),
        name="ranks",
    )(top_idx, base)


def _expert_kernel(blk0_ref, nblk_ref, tab_hbm, h_hbm, wg_ref, wu_ref, wd_ref, out_hbm,
                   tab, wgu_s, wd_s, xb_s, xbuf, ybuf, gsem, ssem, tsem, *, n_tok, d_expert):
    n_real_rows = TOP_K * n_tok
    d_model = xb_s.shape[1]
    sl = d_model // LANES
    e = pl.program_id(0)
    nb = nblk_ref[e]
    b0 = blk0_ref[e]

    def tab_copy(row):
        slot = row & (TAB_RING - 1)
        return pltpu.make_async_copy(tab_hbm.at[pl.ds(row, 1), :], tab.at[pl.ds(slot, 1), :], tsem.at[slot])

    def slab(ref, row):
        return ref.at[pl.ds(pl.multiple_of(row * sl, sl), sl), :]

    def start_gather(g):
        buf = g & (MOE_BUFS - 1)
        trow = (g + 1) & (TAB_RING - 1)
        for r in range(MOE_ROWS):
            tok = tab[trow, r] & (n_tok - 1)
            pltpu.make_async_copy(slab(h_hbm, tok), slab(xbuf.at[buf], r), gsem.at[buf]).start(priority=1)

    def start_scatter(g):
        buf = g & (MOE_BUFS - 1)
        trow = (g + 1) & (TAB_RING - 1)
        for r in range(MOE_ROWS):
            pltpu.make_async_copy(slab(ybuf.at[buf], r), slab(out_hbm, tab[trow, r]),
                                  ssem.at[buf]).start(priority=1)

    def wait_gather(g):
        buf = g & (MOE_BUFS - 1)
        pltpu.make_async_copy(h_hbm.at[pl.ds(0, MOE_ROWS * sl), :], xbuf.at[buf], gsem.at[buf]).wait()

    def wait_scatter(g):
        buf = g & (MOE_BUFS - 1)
        pltpu.make_async_copy(ybuf.at[buf], out_hbm.at[pl.ds(0, MOE_ROWS * sl), :], ssem.at[buf]).wait()

    @pl.when(e == 0)
    def _():
        for row in range(MOE_AHEAD + 2):
            tab_copy(row).start()
        ybuf[...] = jnp.zeros_like(ybuf)
        for row in range(MOE_AHEAD + 1):
            tab_copy(row).wait()
        for buf in range(MOE_BUFS - 1):
            for r in range(MOE_ROWS):
                pltpu.make_async_copy(slab(ybuf.at[buf], r), slab(out_hbm, n_real_rows + buf * MOE_ROWS + r),
                                      ssem.at[buf]).start(priority=1)
        for g in range(MOE_AHEAD):
            start_gather(g)

    @pl.when(nb > 0)
    def _():
        wgu_s[:, :d_expert] = wg_ref[0].astype(BF16)
        wgu_s[:, d_expert:] = wu_ref[0].astype(BF16)
        wd_s[...] = wd_ref[0].astype(BF16)

        def block(b, _):
            g = b0 + b
            tab_copy(g + MOE_AHEAD + 2).start()
            tab_copy(g + MOE_AHEAD + 1).wait()
            wait_gather(g)
            wait_scatter(g - MOE_BUFS)
            xb_s[...] = _slab_load(xbuf.at[g & (MOE_BUFS - 1)], MOE_ROWS, d_model).astype(BF16)
            start_gather(g + MOE_AHEAD)
            start_scatter(g - 1)
            gu = _dot(xb_s[...], wgu_s[...])
            act = gu[:, :d_expert]
            hid = (act * jax.nn.sigmoid(act) * gu[:, d_expert:]).astype(BF16)
            _slab_store(ybuf.at[g & (MOE_BUFS - 1)], _dot(hid, wd_s[...]))
            return 0

        lax.fori_loop(0, nb, block, 0)

    @pl.when(e == pl.num_programs(0) - 1)
    def _():
        n_act = b0 + nb
        start_scatter(n_act - 1)
        tab_copy(n_act + MOE_AHEAD + 1).wait()
        for i in range(MOE_BUFS):
            wait_scatter(n_act - 1 - i)
        for i in range(MOE_AHEAD):
            wait_gather(n_act + i)


def _experts(blk0, nblk, dest, h1_slab, w_gate, w_up, w_down):
    n_experts, d_model, d_expert = w_gate.shape
    sl = d_model // LANES
    n_tok = h1_slab.shape[0] // sl
    n_assign = TOP_K * n_tok
    n_tab = n_assign // MOE_ROWS + n_experts + MOE_AHEAD + 2
    assert n_tok & (n_tok - 1) == 0 and MOE_AHEAD + 3 <= TAB_RING and MOE_AHEAD < MOE_BUFS
    rows = jnp.arange(n_tab * MOE_ROWS, dtype=jnp.int32)
    pad = n_assign + ((rows // MOE_ROWS - 1) % MOE_BUFS) * MOE_ROWS + rows % MOE_ROWS
    tab = pad.at[dest.reshape(n_assign) + MOE_ROWS].set(jnp.arange(n_assign, dtype=jnp.int32),
                                                        unique_indices=True)
    return pl.pallas_call(
        functools.partial(_expert_kernel, n_tok=n_tok, d_expert=d_expert),
        grid_spec=pltpu.PrefetchScalarGridSpec(
            num_scalar_prefetch=2,
            grid=(n_experts,),
            in_specs=[pl.BlockSpec(memory_space=pl.ANY),
                      pl.BlockSpec(memory_space=pl.ANY),
                      pl.BlockSpec((1, d_model, d_expert), lambda e, *_: (e, 0, 0)),
                      pl.BlockSpec((1, d_model, d_expert), lambda e, *_: (e, 0, 0)),
                      pl.BlockSpec((1, d_expert, d_model), lambda e, *_: (e, 0, 0))],
            out_specs=pl.BlockSpec(memory_space=pl.ANY),
            scratch_shapes=[pltpu.SMEM((TAB_RING, MOE_ROWS), jnp.int32),
                            pltpu.VMEM((d_model, 2 * d_expert), BF16),
                            pltpu.VMEM((d_expert, d_model), BF16),
                            pltpu.VMEM((MOE_ROWS, d_model), BF16),
                            pltpu.VMEM((MOE_BUFS, MOE_ROWS * sl, LANES), F32),
                            pltpu.VMEM((MOE_BUFS, MOE_ROWS * sl, LANES), F32),
                            pltpu.SemaphoreType.DMA((MOE_BUFS,)),
                            pltpu.SemaphoreType.DMA((MOE_BUFS,)),
                            pltpu.SemaphoreType.DMA((TAB_RING,))],
        ),
        out_shape=jax.ShapeDtypeStruct(((n_assign + MOE_BUFS * MOE_ROWS) * sl, LANES), F32),
        compiler_params=_cparams(("arbitrary",)),
        name="experts",
    )(blk0, nblk, tab.reshape(n_tab, MOE_ROWS), h1_slab, w_gate, w_up, w_down)


def _final_kernel(*refs, d_expert):
    h_ref, wt_ref = refs[0], refs[1]
    routed_refs = refs[2:2 + TOP_K]
    wgu_ref, wd_ref, g_ref, b_ref, o_ref = refs[2 + TOP_K:]
    h = h_ref[...]
    gu = _dot(h.astype(BF16), wgu_ref[...])
    act = gu[:, :d_expert]
    hid = (act * jax.nn.sigmoid(act) * gu[:, d_expert:]).astype(BF16)
    ff = _dot(hid, wd_ref[...])
    wt = wt_ref[...]
    rows, d_model = h.shape
    for k in range(TOP_K):
        ff += _slab_load(routed_refs[k], rows, d_model) * wt[:, k:k + 1]
    o_ref[...] = _layer_norm(ALPHA * h + ff, g_ref[...], b_ref[...])


def _final(h1, wt, routed, w_gu, w_d, ln_g, ln_b, tm):
    t, d_model = h1.shape
    d_expert = w_d.shape[0]
    per_k = t // tm
    row = lambda w: pl.BlockSpec((tm, w), lambda i: (i, 0))
    const = lambda a: pl.BlockSpec(a.shape, lambda i: (0, 0))
    routed_specs = [pl.BlockSpec((tm * (d_model // LANES), LANES), lambda i, k=k: (k * per_k + i, 0))
                    for k in range(TOP_K)]
    return pl.pallas_call(
        functools.partial(_final_kernel, d_expert=d_expert),
        grid=(t // tm,),
        in_specs=[row(d_model), row(TOP_K)] + routed_specs + [const(w_gu), const(w_d), const(ln_g), const(ln_b)],
        out_specs=row(d_model),
        out_shape=jax.ShapeDtypeStruct((t, d_model), F32),
        compiler_params=_cparams(("arbitrary",)),
        name="final",
    )(h1, wt, *([routed] * TOP_K), w_gu, w_d, ln_g, ln_b)


def _layer(h, mem, w_in, w_mem_kv, sinks, w_a, w_b, w_c, w_out, ln1_g, ln1_b, router_w, router_bias,
           w_eg, w_eu, w_ed, w_sg, w_su, w_sd, ln2_g, ln2_b):
    bsz, seq_len, d_model = h.shape
    n_tok = bsz * seq_len
    n_mem = mem.shape[1]
    n_experts = router_w.shape[1]
    x2 = h.reshape(n_tok, d_model)
    xb = x2.astype(BF16)

    o_qa, o_ka, o_va = 0, WIDTH_A, 2 * WIDTH_A
    o_qb = 3 * WIDTH_A
    o_kb = o_qb + WIDTH_QB
    o_vb = o_kb + WIDTH_KVB
    o_qc = o_vb + WIDTH_KVB
    o_gate = o_qc + WIDTH_C
    wb16 = w_in.astype(BF16)
    cols = lambda o, n: wb16[:, o:o + n]
    chunks_a = WIDTH_A // LANES
    scale_a = HEAD_DIM_A ** -0.5
    tabs_a = _rope_tables(seq_len, HEAD_DIM_A)
    qkv_a = _proj(xb, cols(0, 3 * WIDTH_A),
                  ((("rope128", scale_a),) * chunks_a, (("rope128", 1.0),) * chunks_a,
                   (("plain", 1.0),) * chunks_a),
                  WIDTH_A, BF16, tabs_a, seq_len, dilations=DILATIONS_A)

    hb = HEAD_DIM_B
    dup = lambda o: [cols(o + n * hb, hb) for n in range(N_KV_HEADS_B) for _ in range(2)]
    w_qkv_b = jnp.concatenate([cols(o_qb, WIDTH_QB)] + dup(o_kb) + dup(o_vb), axis=1)
    q_chunks = WIDTH_QB // LANES
    modes_b = ((("rope64", HEAD_DIM_B ** -0.5),) * q_chunks + (("rope64", 1.0),) * N_KV_HEADS_B
               + (("plain", 1.0),) * N_KV_HEADS_B,)
    qkv_b = _proj(xb, w_qkv_b, modes_b, w_qkv_b.shape[1], BF16, _rope_tables(seq_len, HEAD_DIM_B), seq_len)

    q_c = _proj(xb, cols(o_qc, WIDTH_C), ((("plain", HEAD_DIM_C ** -0.5),) * (WIDTH_C // LANES),),
                WIDTH_C, BF16)
    gate_tn = 1536
    gates = _proj(xb, cols(o_gate, 3 * d_model),
                  ((("sigmoid", 1.0),) * (gate_tn // LANES),) * (3 * d_model // gate_tn), gate_tn, BF16)
    mkv = _proj(mem.reshape(bsz * n_mem, d_model).astype(BF16), w_mem_kv.astype(BF16),
                ((("plain", 1.0),) * (2 * WIDTH_C // LANES),), 2 * WIDTH_C, BF16)

    tq = min(512, seq_len // max(DILATIONS_A))
    oas, lses = [], []
    for d, qkv_g in zip(DILATIONS_A, qkv_a):
        first = (jnp.arange(n_tok // tq, dtype=jnp.int32) % (seq_len // d // tq) == 0).astype(jnp.int32)
        o_g, lse_g = _attn_a(qkv_g.reshape(n_tok, 3 * GROUP_WIDTH_A), first, tq)
        oas.append(o_g.reshape(bsz, d, seq_len // d, GROUP_WIDTH_A))
        lses.append(lse_g.reshape(bsz, d, seq_len // d, GROUP_WIDTH_A))

    yb = _attn_b(qkv_b, sinks.astype(F32), seq_len, tq)
    yc = _attn_c(q_c, mkv, seq_len, n_mem, tq)

    h1, h1_slab = _merge(oas, lses, yb, yc, gates, x2, w_a.astype(BF16), w_b.astype(BF16), w_c.astype(BF16),
                         w_out.astype(BF16), ln1_g.reshape(1, d_model), ln1_b.reshape(1, d_model), seq_len,
                         tm=256)

    rw_t = router_w.T
    rw_hi = rw_t.astype(BF16)
    rw_lo = (rw_t - rw_hi.astype(F32)).astype(BF16)
    top_idx, top_w, cnt = _router(h1, rw_hi, rw_lo, router_bias.reshape(n_experts, 1).astype(F32), tm=256)
    counts = jnp.sum(cnt, axis=1).astype(jnp.int32)
    nblk = (counts + MOE_ROWS - 1) // MOE_ROWS
    blk0 = jnp.cumsum(nblk) - nblk
    dest = _ranks(top_idx, (blk0 * MOE_ROWS).astype(F32).reshape(n_experts, 1), tm=256)
    routed = _experts(blk0, nblk, dest, h1_slab, w_eg, w_eu, w_ed)
    w_sgu = jnp.concatenate([w_sg, w_su], axis=1).astype(BF16)
    out = _final(h1, top_w.T, routed, w_sgu, w_sd.astype(BF16), ln2_g.reshape(1, d_model),
                 ln2_b.reshape(1, d_model), tm=256)
    return out.reshape(bsz, seq_len, d_model)


def kernel(x, mem, w_in, w_mem_kv, attn_sinks, w_branch_a, w_branch_b, w_branch_c, w_out, ln1_g, ln1_b,
           router_w, router_bias, w_exp_gate, w_exp_up, w_exp_down, w_sh_gate, w_sh_up, w_sh_down,
           ln2_g, ln2_b):
    h = x
    for layer in range(DEPTH):
        h = _layer(h, mem, w_in[layer], w_mem_kv[layer], attn_sinks[layer], w_branch_a[layer],
                   w_branch_b[layer], w_branch_c[layer], w_out[layer], ln1_g[layer], ln1_b[layer],
                   router_w[layer], router_bias[layer], w_exp_gate[layer], w_exp_up[layer],
                   w_exp_down[layer], w_sh_gate[layer], w_sh_up[layer], w_sh_down[layer],
                   ln2_g[layer], ln2_b[layer])
    return h
```

```python
import functools

import jax
import jax.numpy as jnp
from jax import lax
from jax.experimental import pallas as pl
from jax.experimental.pallas import tpu as pltpu

F32 = jnp.float32
BF16 = jnp.bfloat16
U32 = jnp.uint32

DEPTH = 1
HEAD_DIM_A = 128
DILATIONS_A = (1, 4, 16)
N_BACK_A = 128
HEADS_PER_GROUP_A = 4
GROUP_WIDTH_A = HEADS_PER_GROUP_A * HEAD_DIM_A
WIDTH_A = len(DILATIONS_A) * GROUP_WIDTH_A
HEAD_DIM_B = 64
N_Q_HEADS_B = 16
N_KV_HEADS_B = 2
WIDTH_QB = N_Q_HEADS_B * HEAD_DIM_B
WIDTH_KVB = N_KV_HEADS_B * HEAD_DIM_B
N_BACK_B = 127
N_HEADS_C = 4
HEAD_DIM_C = 256
WIDTH_C = N_HEADS_C * HEAD_DIM_C
ROPE_THETA = 10000.0
TOP_K = 8
N_EXPERT_GROUPS = 8
TOPK_EXPERT_GROUPS = 4
ROUTED_SCALE = 2.5
ALPHA = (2 * DEPTH) ** 0.25
LN_EPS = 1e-5

LANES = 128
SLAB_COLS = 2 * LANES
ATT_BLK = 128
MOE_ROWS = 128
MOE_BUFS = 4
MOE_AHEAD = 2
TAB_RING = 8
VMEM_LIMIT = 56 * 1024 * 1024
NEG = -1e30


def _cparams(sem):
    return pltpu.CompilerParams(dimension_semantics=sem, vmem_limit_bytes=VMEM_LIMIT)


def _dot(a, b):
    return jnp.dot(a, b, preferred_element_type=F32)


def _dot_nt(a, b):
    return lax.dot_general(a, b, (((1,), (1,)), ((), ())), preferred_element_type=F32)


def _rope_chunk(a, cos, sin, head_dim):
    if head_dim == 128:
        partner = pltpu.roll(a, 64, 1)
    else:
        lane = lax.broadcasted_iota(jnp.int32, a.shape, 1)
        partner = jnp.where((lane % 64) < 32, pltpu.roll(a, 96, 1), pltpu.roll(a, 32, 1))
    return a * cos + partner * sin


def _proj_kernel(*refs, modes, has_tables, dilations):
    n_out = 1 if dilations is None else len(dilations)
    x_ref, w_ref = refs[:2]
    cos_ref, sin_ref = refs[2:4] if has_tables else (None, None)
    o_refs = refs[2 + 2 * has_tables:2 + 2 * has_tables + n_out]
    stage = refs[-1] if dilations is not None else None
    j = pl.program_id(0)
    acc = _dot(x_ref[...], w_ref[...])
    tm = acc.shape[0]

    def store(c, a):
        if dilations is None:
            o_refs[0][:, c * LANES:(c + 1) * LANES] = a.astype(o_refs[0].dtype)
            return
        per_group = len(modes[0]) // n_out
        gi, cg = divmod(c, per_group)
        d, o_ref = dilations[gi], o_refs[gi]
        lanes = slice(cg * LANES, (cg + 1) * LANES)
        if d == 1:
            o_ref[0, 0, :, lanes] = a.astype(o_ref.dtype)
        else:
            stage[c] = a
            for r in range(d):
                o_ref[0, r, :, lanes] = stage[c, pl.ds(r, tm // d, stride=d), :].astype(o_ref.dtype)

    def epilogue(chunk_modes):
        for c, (kind, scale) in enumerate(chunk_modes):
            a = acc[:, c * LANES:(c + 1) * LANES]
            if kind == "rope128":
                a = _rope_chunk(a, cos_ref[...], sin_ref[...], 128)
            elif kind == "rope64":
                a = _rope_chunk(a, cos_ref[...], sin_ref[...], 64)
            elif kind == "sigmoid":
                a = jax.nn.sigmoid(a)
            if scale != 1.0:
                a = a * scale
            store(c, a)

    if all(m == modes[0] for m in modes):
        epilogue(modes[0])
    else:
        for jj, chunk_modes in enumerate(modes):
            pl.when(j == jj)(functools.partial(epilogue, chunk_modes))


def _proj(x, w, modes, tn, out_dtype, tables=None, seq_len=None, tm=1024, dilations=None):
    m, k = x.shape
    n = w.shape[1]
    tm = min(tm, m)
    assert m % tm == 0 and n % tn == 0 and len(modes) == n // tn
    in_specs = [pl.BlockSpec((tm, k), lambda j, i: (i, 0)),
                pl.BlockSpec((k, tn), lambda j, i: (0, j))]
    args = [x, w]
    if tables is not None:
        per_seq = seq_len // tm
        tab_spec = pl.BlockSpec((tm, LANES), lambda j, i: (i % per_seq, 0))
        in_specs += [tab_spec, tab_spec]
        args += list(tables)
    scratch = []
    if dilations is None:
        out_specs = pl.BlockSpec((tm, tn), lambda j, i: (i, j))
        out_shape = jax.ShapeDtypeStruct((m, n), out_dtype)
    else:
        per_seq = seq_len // tm
        gw = tn // len(dilations)
        out_specs = [pl.BlockSpec((1, d, tm // d, gw), lambda j, i: (i // per_seq, 0, i % per_seq, j))
                     for d in dilations]
        out_shape = [jax.ShapeDtypeStruct((m // seq_len, d, seq_len // d, (n // tn) * gw), out_dtype)
                     for d in dilations]
        scratch = [pltpu.VMEM((tn // LANES, tm, LANES), F32)]
    return pl.pallas_call(
        functools.partial(_proj_kernel, modes=modes, has_tables=tables is not None, dilations=dilations),
        grid=(n // tn, m // tm),
        in_specs=in_specs,
        out_specs=out_specs,
        out_shape=out_shape,
        scratch_shapes=scratch,
        compiler_params=_cparams(("arbitrary", "arbitrary")),
        name="proj",
    )(*args)


def _rope_tables(seq_len, head_dim):
    half = head_dim // 2
    inv_freq = jnp.power(ROPE_THETA, -jnp.arange(half, dtype=F32) * 2.0 / head_dim)
    ang = jnp.arange(seq_len).astype(F32)[:, None] * inv_freq[None, :]
    cos, sin = jnp.cos(ang), jnp.sin(ang)
    reps = LANES // head_dim
    cos_t = jnp.tile(jnp.concatenate([cos, cos], axis=-1), (1, reps))
    sin_t = jnp.tile(jnp.concatenate([-sin, sin], axis=-1), (1, reps))
    return cos_t, sin_t


def _softmax_rows(s):
    m = jnp.max(s, axis=-1, keepdims=True)
    p = jnp.exp(s - m)
    l = jnp.sum(p, axis=-1, keepdims=True)
    return m, p, l


def _attn_a_kernel(first_ref, q_ref, kc_ref, vc_ref, kp_ref, vp_ref, o_ref, lse_ref, *, tq):
    c = pl.program_id(0)
    not_first = first_ref[c] == 0
    row = lax.broadcasted_iota(jnp.int32, (ATT_BLK, 2 * ATT_BLK), 0)
    col = lax.broadcasted_iota(jnp.int32, (ATT_BLK, 2 * ATT_BLK), 1)
    band = (col >= row) & (col <= row + N_BACK_A)
    band_first = band & ((col >= ATT_BLK) | not_first)
    for h in range(HEADS_PER_GROUP_A):
        hs = slice(h * HEAD_DIM_A, (h + 1) * HEAD_DIM_A)
        for i in range(tq // ATT_BLK):
            rows = slice(i * ATT_BLK, (i + 1) * ATT_BLK)
            q = q_ref[rows, hs]
            if i == 0:
                k = jnp.concatenate([kp_ref[:, hs], kc_ref[rows, hs]], axis=0)
                v = jnp.concatenate([vp_ref[:, hs], vc_ref[rows, hs]], axis=0)
                mask = band_first
            else:
                kv_rows = slice((i - 1) * ATT_BLK, (i + 1) * ATT_BLK)
                k = kc_ref[kv_rows, hs]
                v = vc_ref[kv_rows, hs]
                mask = band
            s = jnp.where(mask, _dot_nt(q, k), NEG)
            m, p, l = _softmax_rows(s)
            o = _dot(p.astype(BF16), v) / l
            o_ref[rows, hs] = o.astype(o_ref.dtype)
            lse_ref[rows, hs] = jnp.broadcast_to(m + jnp.log(l), (ATT_BLK, HEAD_DIM_A))


def _attn_a(qkv, first, tq):
    r = qkv.shape[0]
    gw = GROUP_WIDTH_A
    sub = tq // ATT_BLK
    cur = lambda col: pl.BlockSpec((tq, gw), lambda c, f: (c, col))
    prev = lambda col: pl.BlockSpec((ATT_BLK, gw), lambda c, f: (jnp.maximum(c * sub - 1, 0), col))
    return pl.pallas_call(
        functools.partial(_attn_a_kernel, tq=tq),
        grid_spec=pltpu.PrefetchScalarGridSpec(
            num_scalar_prefetch=1,
            grid=(r // tq,),
            in_specs=[cur(0), cur(1), cur(2), prev(1), prev(2)],
            out_specs=[pl.BlockSpec((tq, gw), lambda c, f: (c, 0)),
                       pl.BlockSpec((tq, gw), lambda c, f: (c, 0))],
        ),
        out_shape=[jax.ShapeDtypeStruct((r, gw), BF16), jax.ShapeDtypeStruct((r, gw), F32)],
        compiler_params=_cparams(("arbitrary",)),
        name="attn_a",
    )(first, qkv, qkv, qkv, qkv, qkv)


def _attn_b_kernel(sink_ref, q_ref, kvc_ref, kvp_ref, o_ref, *, tq, chunks_per_seq):
    c = pl.program_id(0)
    not_first = (c % chunks_per_seq) != 0
    row = lax.broadcasted_iota(jnp.int32, (ATT_BLK, 2 * ATT_BLK), 0)
    col = lax.broadcasted_iota(jnp.int32, (ATT_BLK, 2 * ATT_BLK), 1)
    band = (col >= row + 1) & (col <= row + 1 + N_BACK_B)
    band_first = band & ((col >= ATT_BLK) | not_first)
    lane_kv = lax.broadcasted_iota(jnp.int32, (2 * ATT_BLK, LANES), 1) < HEAD_DIM_B
    lane_o = lax.broadcasted_iota(jnp.int32, (ATT_BLK, LANES), 1) < HEAD_DIM_B
    pairs = (N_Q_HEADS_B // N_KV_HEADS_B) // 2
    for i in range(tq // ATT_BLK):
        rows = slice(i * ATT_BLK, (i + 1) * ATT_BLK)
        if i == 0:
            kv = jnp.concatenate([kvp_ref[...], kvc_ref[rows, :]], axis=0)
            mask = band_first
        else:
            kv = kvc_ref[(i - 1) * ATT_BLK:(i + 1) * ATT_BLK, :]
            mask = band
        for n in range(N_KV_HEADS_B):
            kk = kv[:, n * LANES:(n + 1) * LANES]
            vv = kv[:, (N_KV_HEADS_B + n) * LANES:(N_KV_HEADS_B + n + 1) * LANES]
            zero = jnp.zeros_like(kk)
            k_bd = jnp.concatenate([jnp.where(lane_kv, kk, zero), jnp.where(lane_kv, zero, kk)], axis=0)
            v_bd = jnp.concatenate([jnp.where(lane_kv, vv, zero), jnp.where(lane_kv, zero, vv)], axis=0)
            for jp in range(pairs):
                c0 = n * (WIDTH_QB // N_KV_HEADS_B) + jp * LANES
                q = q_ref[rows, c0:c0 + LANES]
                s2 = _dot_nt(q, k_bd)
                ps, invs = [], []
                for hh in range(2):
                    sink = sink_ref[n * 2 * pairs + jp * 2 + hh]
                    s = jnp.where(mask, s2[:, hh * 2 * ATT_BLK:(hh + 1) * 2 * ATT_BLK], NEG)
                    m = jnp.maximum(jnp.max(s, axis=-1, keepdims=True), sink)
                    p = jnp.exp(s - m)
                    denom = jnp.sum(p, axis=-1, keepdims=True) + jnp.exp(sink - m)
                    ps.append(p.astype(BF16))
                    invs.append(1.0 / denom)
                o = _dot(jnp.concatenate(ps, axis=1), v_bd)
                o = o * jnp.where(lane_o, invs[0], invs[1])
                o_ref[rows, c0:c0 + LANES] = o.astype(o_ref.dtype)


def _attn_b(qkv, sinks, seq_len, tq):
    t = qkv.shape[0]
    sub = tq // ATT_BLK
    kv_col = WIDTH_QB // (4 * LANES)
    return pl.pallas_call(
        functools.partial(_attn_b_kernel, tq=tq, chunks_per_seq=seq_len // tq),
        grid=(t // tq,),
        in_specs=[pl.BlockSpec(memory_space=pltpu.SMEM),
                  pl.BlockSpec((tq, WIDTH_QB), lambda c: (c, 0)),
                  pl.BlockSpec((tq, 4 * LANES), lambda c: (c, kv_col)),
                  pl.BlockSpec((ATT_BLK, 4 * LANES), lambda c: (jnp.maximum(c * sub - 1, 0), kv_col))],
        out_specs=pl.BlockSpec((tq, WIDTH_QB), lambda c: (c, 0)),
        out_shape=jax.ShapeDtypeStruct((t, WIDTH_QB), BF16),
        compiler_params=_cparams(("arbitrary",)),
        name="attn_b",
    )(sinks, qkv, qkv, qkv)


def _attn_c_kernel(q_ref, mkv_ref, o_ref, *, tq, sub):
    for h in range(N_HEADS_C):
        hs = slice(h * HEAD_DIM_C, (h + 1) * HEAD_DIM_C)
        mk = mkv_ref[:, hs]
        mv = mkv_ref[:, WIDTH_C + h * HEAD_DIM_C:WIDTH_C + (h + 1) * HEAD_DIM_C]
        for i in range(tq // sub):
            rows = slice(i * sub, (i + 1) * sub)
            s = _dot_nt(q_ref[rows, hs], mk)
            _, p, l = _softmax_rows(s)
            o_ref[rows, hs] = (_dot(p.astype(BF16), mv) / l).astype(o_ref.dtype)


def _attn_c(qc, mkv, seq_len, n_mem, tq):
    t = qc.shape[0]
    per_seq = seq_len // tq
    return pl.pallas_call(
        functools.partial(_attn_c_kernel, tq=tq, sub=min(tq, 256)),
        grid=(t // tq,),
        in_specs=[pl.BlockSpec((tq, WIDTH_C), lambda c: (c, 0)),
                  pl.BlockSpec((n_mem, 2 * WIDTH_C), lambda c: (c // per_seq, 0))],
        out_specs=pl.BlockSpec((tq, WIDTH_C), lambda c: (c, 0)),
        out_shape=jax.ShapeDtypeStruct((t, WIDTH_C), BF16),
        compiler_params=_cparams(("arbitrary",)),
        name="attn_c",
    )(qc, mkv)


def _slab_store(ref, val):
    rows, width = val.shape
    n = width // (2 * LANES)
    for p in range(n):
        lo = lax.bitcast_convert_type(val[:, (2 * p) * LANES:(2 * p + 1) * LANES].astype(BF16).astype(F32), U32)
        hi = lax.bitcast_convert_type(val[:, (2 * p + 1) * LANES:(2 * p + 2) * LANES].astype(BF16).astype(F32), U32)
        ref[pl.ds(p, rows, stride=n), :] = (lo >> 16) | (hi & jnp.uint32(0xFFFF0000))


def _slab_load(ref, rows, width):
    n = width // (2 * LANES)
    chunks = []
    for p in range(n):
        w = ref[pl.ds(p, rows, stride=n), :]
        chunks.append(lax.bitcast_convert_type(w << 16, F32))
        chunks.append(lax.bitcast_convert_type(w & jnp.uint32(0xFFFF0000), F32))
    return jnp.concatenate(chunks, axis=1)


def _layer_norm(z, g, b):
    mu = jnp.mean(z, axis=-1, keepdims=True)
    zc = z - mu
    var = jnp.mean(zc * zc, axis=-1, keepdims=True)
    return zc * lax.rsqrt(var + LN_EPS) * g + b


def _merge_kernel(*refs, d_model):
    n_grp = len(DILATIONS_A)
    oa_refs, lse_refs = refs[:n_grp], refs[n_grp:2 * n_grp]
    (yb_ref, yc_ref, gate_ref, x_ref, wa_ref, wb_ref, wc_ref, wo_ref, g_ref, b_ref,
     h_ref, hs_ref, oa_s, lse_s) = refs[2 * n_grp:]
    tm = x_ref.shape[0]

    def natural(ref, stage, g):
        d = DILATIONS_A[g]
        if d == 1:
            return ref[0, 0].astype(F32)
        chunks = GROUP_WIDTH_A // LANES
        for r in range(d):
            for c in range(chunks):
                stage[g, c, pl.ds(r, tm // d, stride=d), :] = ref[0, r, :, c * LANES:(c + 1) * LANES].astype(F32)
        return jnp.concatenate([stage[g, c] for c in range(chunks)], axis=1)

    lses = [natural(lse_refs[g], lse_s, g) for g in range(n_grp)]
    top = functools.reduce(jnp.maximum, lses)
    es = [jnp.exp(l - top) for l in lses]
    tot = functools.reduce(jnp.add, es)
    ya = functools.reduce(jnp.add, [e * natural(oa_refs[g], oa_s, g) for g, e in enumerate(es)]) / tot
    gate = lambda g: gate_ref[:, g * d_model:(g + 1) * d_model].astype(F32)
    merged = gate(0) * _dot(ya.astype(BF16), wa_ref[...])
    merged += gate(1) * _dot(yb_ref[...], wb_ref[...])
    merged += gate(2) * _dot(yc_ref[...], wc_ref[...])
    mix = _dot(merged.astype(BF16), wo_ref[...])
    h = _layer_norm(ALPHA * x_ref[...] + mix, g_ref[...], b_ref[...])
    h_ref[...] = h
    _slab_store(hs_ref, h)


def _merge(oas, lses, yb, yc, gates, x2, wa, wb, wc, wo, ln_g, ln_b, seq_len, tm):
    t, d_model = x2.shape
    per_seq = seq_len // tm
    row = lambda w: pl.BlockSpec((tm, w), lambda i: (i, 0))
    grp = lambda d: pl.BlockSpec((1, d, tm // d, GROUP_WIDTH_A), lambda i: (i // per_seq, 0, i % per_seq, 0))
    const = lambda a: pl.BlockSpec(a.shape, lambda i: (0, 0), pipeline_mode=pl.Buffered(1))
    n_grp = len(DILATIONS_A)
    return pl.pallas_call(
        functools.partial(_merge_kernel, d_model=d_model),
        grid=(t // tm,),
        in_specs=[grp(d) for d in DILATIONS_A] * 2 + [row(WIDTH_QB), row(WIDTH_C),
                  row(3 * d_model), row(d_model), const(wa), const(wb), const(wc), const(wo),
                  const(ln_g), const(ln_b)],
        out_specs=[row(d_model), pl.BlockSpec((tm * (d_model // SLAB_COLS), LANES), lambda i: (i, 0))],
        out_shape=[jax.ShapeDtypeStruct((t, d_model), F32),
                   jax.ShapeDtypeStruct((t * (d_model // SLAB_COLS), LANES), U32)],
        scratch_shapes=[pltpu.VMEM((n_grp, GROUP_WIDTH_A // LANES, tm, LANES), F32)] * 2,
        compiler_params=_cparams(("arbitrary",)),
        name="merge",
    )(*oas, *lses, yb, yc, gates, x2, wa, wb, wc, wo, ln_g, ln_b)


def _router_kernel(h_ref, whi_ref, wlo_ref, bias_ref, idx_ref, wgt_ref, cnt_ref, *, n_experts):
    h = h_ref[...]
    h_hi = h.astype(BF16)
    h_lo = (h - h_hi.astype(F32)).astype(BF16)
    logits = _dot_nt(whi_ref[...], h_hi) + _dot_nt(whi_ref[...], h_lo) + _dot_nt(wlo_ref[...], h_hi)
    scores = jax.nn.sigmoid(logits)
    biased = scores + bias_ref[...]
    tm = scores.shape[1]
    per_group = n_experts // N_EXPERT_GROUPS
    neg_inf = -jnp.inf

    b3 = biased.reshape(N_EXPERT_GROUPS, per_group, tm)
    i3 = lax.broadcasted_iota(jnp.int32, b3.shape, 1)
    m1 = jnp.max(b3, axis=1, keepdims=True)
    a1 = jnp.min(jnp.where(b3 == m1, i3, per_group), axis=1, keepdims=True)
    m2 = jnp.max(jnp.where(i3 == a1, neg_inf, b3), axis=1, keepdims=True)
    gscore = (m1 + m2).reshape(N_EXPERT_GROUPS, tm)

    gi = lax.broadcasted_iota(jnp.int32, gscore.shape, 0)
    chosen = jnp.zeros(gscore.shape, jnp.int32)
    for _ in range(TOPK_EXPERT_GROUPS):
        gm = jnp.max(gscore, axis=0, keepdims=True)
        ga = jnp.min(jnp.where(gscore == gm, gi, N_EXPERT_GROUPS), axis=0, keepdims=True)
        hit = gi == ga
        chosen = jnp.where(hit, 1, chosen)
        gscore = jnp.where(hit, neg_inf, gscore)

    cur = jnp.where(chosen.reshape(N_EXPERT_GROUPS, 1, tm) > 0, b3, neg_inf).reshape(n_experts, tm)
    ei = lax.broadcasted_iota(jnp.int32, cur.shape, 0)
    idxs, wgts = [], []
    member = jnp.zeros(cur.shape, F32)
    for _ in range(TOP_K):
        m = jnp.max(cur, axis=0, keepdims=True)
        a = jnp.min(jnp.where(cur == m, ei, n_experts), axis=0, keepdims=True)
        hit = ei == a
        idxs.append(a)
        wgts.append(jnp.sum(jnp.where(hit, scores, 0.0), axis=0, keepdims=True))
        member = jnp.where(hit, 1.0, member)
        cur = jnp.where(hit, neg_inf, cur)
    wsum = functools.reduce(jnp.add, wgts)
    idx_ref[...] = jnp.concatenate(idxs, axis=0)
    wgt_ref[...] = jnp.concatenate(wgts, axis=0) / wsum * ROUTED_SCALE

    @pl.when(pl.program_id(0) == 0)
    def _():
        cnt_ref[...] = jnp.zeros_like(cnt_ref)

    cnt_ref[...] += functools.reduce(
        jnp.add, [member[:, c * LANES:(c + 1) * LANES] for c in range(tm // LANES)])


def _router(h1, w_hi, w_lo, bias, tm):
    t, d_model = h1.shape
    n_experts = w_hi.shape[0]
    const = lambda a: pl.BlockSpec(a.shape, lambda i: (0, 0))
    return pl.pallas_call(
        functools.partial(_router_kernel, n_experts=n_experts),
        grid=(t // tm,),
        in_specs=[pl.BlockSpec((tm, d_model), lambda i: (i, 0)), const(w_hi), const(w_lo), const(bias)],
        out_specs=[pl.BlockSpec((TOP_K, tm), lambda i: (0, i)),
                   pl.BlockSpec((TOP_K, tm), lambda i: (0, i)),
                   pl.BlockSpec((n_experts, LANES), lambda i: (0, 0))],
        out_shape=[jax.ShapeDtypeStruct((TOP_K, t), jnp.int32), jax.ShapeDtypeStruct((TOP_K, t), F32),
                   jax.ShapeDtypeStruct((n_experts, LANES), F32)],
        compiler_params=_cparams(("arbitrary",)),
        name="router",
    )(h1, w_hi, w_lo, bias)


def _rank_kernel(idx_ref, base_ref, dest_ref, carry, *, n_experts):
    @pl.when(pl.program_id(0) == 0)
    def _():
        carry[...] = jnp.zeros_like(carry)

    idx = idx_ref[...]
    tm = idx.shape[1]
    ei = lax.broadcasted_iota(jnp.int32, (n_experts, tm), 0)
    hits = [ei == idx[k:k + 1, :] for k in range(TOP_K)]
    member = functools.reduce(jnp.add, [jnp.where(h, 1.0, 0.0) for h in hits])
    earlier = (lax.broadcasted_iota(jnp.int32, (tm, tm), 0)
               < lax.broadcasted_iota(jnp.int32, (tm, tm), 1))
    before = _dot(member.astype(BF16), jnp.where(earlier, 1.0, 0.0).astype(BF16))
    row = base_ref[...] + carry[...] + before
    dest = [jnp.sum(jnp.where(h, row, 0.0), axis=0, keepdims=True) for h in hits]
    dest_ref[...] = jnp.concatenate(dest, axis=0).astype(jnp.int32)
    carry[...] += jnp.sum(member, axis=1, keepdims=True)


def _ranks(top_idx, base, tm):
    k, t = top_idx.shape
    n_experts = base.shape[0]
    return pl.pallas_call(
        functools.partial(_rank_kernel, n_experts=n_experts),
        grid=(t // tm,),
        in_specs=[pl.BlockSpec((k, tm), lambda i: (0, i)), pl.BlockSpec((n_experts, 1), lambda i: (0, 0))],
        out_specs=pl.BlockSpec((k, tm), lambda i: (0, i)),
        out_shape=jax.ShapeDtypeStruct((k, t), jnp.int32),
        scratch_shapes=[pltpu.VMEM((n_experts, 1), F32)],
        compiler_params=_cparams(("arbitrary",)),
        name="ranks",
    )(top_idx, base)


def _expert_kernel(blk0_ref, nblk_ref, tab_hbm, h_hbm, wg_ref, wu_ref, wd_ref, out_hbm,
                   tab, wgu_s, wd_s, xb_s, xbuf, ybuf, gsem, ssem, tsem, *, n_tok, d_expert):
    n_real_rows = TOP_K * n_tok
    d_model = xb_s.shape[1]
    sl = d_model // SLAB_COLS
    e = pl.program_id(0)
    nb = nblk_ref[e]
    b0 = blk0_ref[e]

    def tab_copy(g):
        slot = (g + 1) & (TAB_RING - 1)
        return pltpu.make_async_copy(tab_hbm.at[pl.ds(g + 1, 1), :], tab.at[pl.ds(slot, 1), :], tsem.at[slot])

    def tab_start(g):
        tab_copy(g).start()

    def tab_wait(g):
        tab_copy(g).wait()

    def slab(ref, row):
        return ref.at[pl.ds(pl.multiple_of(row * sl, sl), sl), :]

    def start_gather(g):
        buf = g & (MOE_BUFS - 1)
        trow = (g + 1) & (TAB_RING - 1)
        for r in range(MOE_ROWS):
            tok = tab[trow, r] & (n_tok - 1)
            pltpu.make_async_copy(slab(h_hbm, tok), slab(xbuf.at[buf], r), gsem.at[buf]).start(priority=1)

    def start_scatter(g):
        buf = g & (MOE_BUFS - 1)
        trow = (g + 1) & (TAB_RING - 1)
        for r in range(MOE_ROWS):
            pltpu.make_async_copy(slab(ybuf.at[buf], r), slab(out_hbm, tab[trow, r]),
                                  ssem.at[buf]).start(priority=1)

    def wait_gather(g):
        buf = g & (MOE_BUFS - 1)
        pltpu.make_async_copy(h_hbm.at[pl.ds(0, MOE_ROWS * sl), :], xbuf.at[buf], gsem.at[buf]).wait()

    def wait_scatter(g):
        buf = g & (MOE_BUFS - 1)
        pltpu.make_async_copy(ybuf.at[buf], out_hbm.at[pl.ds(0, MOE_ROWS * sl), :], ssem.at[buf]).wait()

    @pl.when(e == 0)
    def _():
        for g in range(-1, MOE_AHEAD + 1):
            tab_start(g)
        ybuf[...] = jnp.zeros_like(ybuf)
        for g in range(-1, MOE_AHEAD):
            tab_wait(g)
        for buf in range(MOE_BUFS - 1):
            for r in range(MOE_ROWS):
                pltpu.make_async_copy(slab(ybuf.at[buf], r), slab(out_hbm, n_real_rows + buf * MOE_ROWS + r),
                                      ssem.at[buf]).start(priority=1)
        for g in range(MOE_AHEAD):
            start_gather(g)

    @pl.when(nb > 0)
    def _():
        wgu_s[:, :d_expert] = wg_ref[0].astype(BF16)
        wgu_s[:, d_expert:] = wu_ref[0].astype(BF16)
        wd_s[...] = wd_ref[0].astype(BF16)

        def block(b, _):
            g = b0 + b
            tab_start(g + MOE_AHEAD + 1)
            tab_wait(g + MOE_AHEAD)
            wait_gather(g)
            wait_scatter(g - MOE_BUFS)
            xb_s[...] = _slab_load(xbuf.at[g & (MOE_BUFS - 1)], MOE_ROWS, d_model).astype(BF16)
            start_gather(g + MOE_AHEAD)
            start_scatter(g - 1)
            gu = _dot(xb_s[...], wgu_s[...])
            act = gu[:, :d_expert]
            hid = (act * jax.nn.sigmoid(act) * gu[:, d_expert:]).astype(BF16)
            _slab_store(ybuf.at[g & (MOE_BUFS - 1)], _dot(hid, wd_s[...]))
            return 0

        lax.fori_loop(0, nb, block, 0)

    @pl.when(e == pl.num_programs(0) - 1)
    def _():
        n_act = b0 + nb
        start_scatter(n_act - 1)
        tab_wait(n_act + MOE_AHEAD)
        for i in range(MOE_BUFS):
            wait_scatter(n_act - 1 - i)
        for i in range(MOE_AHEAD):
            wait_gather(n_act + i)


def _experts(counts, dest, h1_slab, w_gate, w_up, w_down):
    n_experts, d_model, d_expert = w_gate.shape
    sl = d_model // SLAB_COLS
    n_tok = h1_slab.shape[0] // sl
    n_assign = TOP_K * n_tok
    assert n_tok & (n_tok - 1) == 0 and MOE_AHEAD + 3 <= TAB_RING and MOE_AHEAD < MOE_BUFS
    nblk = (counts + MOE_ROWS - 1) // MOE_ROWS
    blk_end = jnp.cumsum(nblk)
    blk0 = blk_end - nblk
    starts = jnp.cumsum(counts) - counts
    _, order = lax.sort_key_val(dest.reshape(n_assign), jnp.arange(n_assign, dtype=jnp.int32))
    order = jnp.concatenate([order, jnp.zeros((MOE_ROWS,), jnp.int32)])
    n_tab = n_assign // MOE_ROWS + n_experts + MOE_AHEAD + 2
    g = jnp.arange(n_tab, dtype=jnp.int32) - 1
    ge = jnp.clip(jnp.searchsorted(blk_end, g, side="right"), 0, n_experts - 1).astype(jnp.int32)
    b = g - blk0[ge]
    live = (g >= 0) & (g < blk_end[-1])
    c0 = jnp.where(live, starts[ge] + b * MOE_ROWS, n_assign).astype(jnp.int32)
    nv = jnp.where(live, jnp.minimum(counts[ge] - b * MOE_ROWS, MOE_ROWS), 0).astype(jnp.int32)
    r = jnp.arange(MOE_ROWS, dtype=jnp.int32)[None, :]
    tab = jax.vmap(lambda c: lax.dynamic_slice_in_dim(order, c, MOE_ROWS))(c0)
    tab = jnp.where(r < nv[:, None], tab, n_assign + (g[:, None] % MOE_BUFS) * MOE_ROWS + r)
    return pl.pallas_call(
        functools.partial(_expert_kernel, n_tok=n_tok, d_expert=d_expert),
        grid_spec=pltpu.PrefetchScalarGridSpec(
            num_scalar_prefetch=2,
            grid=(n_experts,),
            in_specs=[pl.BlockSpec(memory_space=pl.ANY),
                      pl.BlockSpec(memory_space=pl.ANY),
                      pl.BlockSpec((1, d_model, d_expert), lambda e, *_: (e, 0, 0)),
                      pl.BlockSpec((1, d_model, d_expert), lambda e, *_: (e, 0, 0)),
                      pl.BlockSpec((1, d_expert, d_model), lambda e, *_: (e, 0, 0))],
            out_specs=pl.BlockSpec(memory_space=pl.ANY),
            scratch_shapes=[pltpu.SMEM((TAB_RING, MOE_ROWS), jnp.int32),
                            pltpu.VMEM((d_model, 2 * d_expert), BF16),
                            pltpu.VMEM((d_expert, d_model), BF16),
                            pltpu.VMEM((MOE_ROWS, d_model), BF16),
                            pltpu.VMEM((MOE_BUFS, MOE_ROWS * sl, LANES), U32),
                            pltpu.VMEM((MOE_BUFS, MOE_ROWS * sl, LANES), U32),
                            pltpu.SemaphoreType.DMA((MOE_BUFS,)),
                            pltpu.SemaphoreType.DMA((MOE_BUFS,)),
                            pltpu.SemaphoreType.DMA((TAB_RING,))],
        ),
        out_shape=jax.ShapeDtypeStruct(((n_assign + MOE_BUFS * MOE_ROWS) * sl, LANES), U32),
        compiler_params=_cparams(("arbitrary",)),
        name="experts",
    )(blk0.astype(jnp.int32), nblk.astype(jnp.int32), tab, h1_slab, w_gate, w_up, w_down)


def _final_kernel(*refs, d_expert):
    h_ref, wt_ref = refs[0], refs[1]
    routed_refs = refs[2:2 + TOP_K]
    wgu_ref, wd_ref, g_ref, b_ref, o_ref = refs[2 + TOP_K:]
    h = h_ref[...]
    gu = _dot(h.astype(BF16), wgu_ref[...])
    act = gu[:, :d_expert]
    hid = (act * jax.nn.sigmoid(act) * gu[:, d_expert:]).astype(BF16)
    ff = _dot(hid, wd_ref[...])
    wt = wt_ref[...]
    rows, d_model = h.shape
    for k in range(TOP_K):
        ff += _slab_load(routed_refs[k], rows, d_model) * wt[:, k:k + 1]
    o_ref[...] = _layer_norm(ALPHA * h + ff, g_ref[...], b_ref[...])


def _final(h1, wt, routed, w_gu, w_d, ln_g, ln_b, tm):
    t, d_model = h1.shape
    d_expert = w_d.shape[0]
    per_k = t // tm
    row = lambda w: pl.BlockSpec((tm, w), lambda i: (i, 0))
    const = lambda a: pl.BlockSpec(a.shape, lambda i: (0, 0))
    routed_specs = [pl.BlockSpec((tm * (d_model // SLAB_COLS), LANES), lambda i, k=k: (k * per_k + i, 0))
                    for k in range(TOP_K)]
    return pl.pallas_call(
        functools.partial(_final_kernel, d_expert=d_expert),
        grid=(t // tm,),
        in_specs=[row(d_model), row(TOP_K)] + routed_specs + [const(w_gu), const(w_d), const(ln_g), const(ln_b)],
        out_specs=row(d_model),
        out_shape=jax.ShapeDtypeStruct((t, d_model), F32),
        compiler_params=_cparams(("arbitrary",)),
        name="final",
    )(h1, wt, *([routed] * TOP_K), w_gu, w_d, ln_g, ln_b)


def _layer(h, mem, w_in, w_mem_kv, sinks, w_a, w_b, w_c, w_out, ln1_g, ln1_b, router_w, router_bias,
           w_eg, w_eu, w_ed, w_sg, w_su, w_sd, ln2_g, ln2_b):
    bsz, seq_len, d_model = h.shape
    n_tok = bsz * seq_len
    n_mem = mem.shape[1]
    n_experts = router_w.shape[1]
    x2 = h.reshape(n_tok, d_model)
    xb = x2.astype(BF16)

    o_qa, o_ka, o_va = 0, WIDTH_A, 2 * WIDTH_A
    o_qb = 3 * WIDTH_A
    o_kb = o_qb + WIDTH_QB
    o_vb = o_kb + WIDTH_KVB
    o_qc = o_vb + WIDTH_KVB
    o_gate = o_qc + WIDTH_C
    wb16 = w_in.astype(BF16)
    cols = lambda o, n: wb16[:, o:o + n]
    chunks_a = WIDTH_A // LANES
    scale_a = HEAD_DIM_A ** -0.5
    tabs_a = _rope_tables(seq_len, HEAD_DIM_A)
    qkv_a = _proj(xb, cols(0, 3 * WIDTH_A),
                  ((("rope128", scale_a),) * chunks_a, (("rope128", 1.0),) * chunks_a,
                   (("plain", 1.0),) * chunks_a),
                  WIDTH_A, BF16, tabs_a, seq_len, dilations=DILATIONS_A)

    hb = HEAD_DIM_B
    dup = lambda o: [cols(o + n * hb, hb) for n in range(N_KV_HEADS_B) for _ in range(2)]
    w_qkv_b = jnp.concatenate([cols(o_qb, WIDTH_QB)] + dup(o_kb) + dup(o_vb), axis=1)
    q_chunks = WIDTH_QB // LANES
    modes_b = ((("rope64", HEAD_DIM_B ** -0.5),) * q_chunks + (("rope64", 1.0),) * N_KV_HEADS_B
               + (("plain", 1.0),) * N_KV_HEADS_B,)
    qkv_b = _proj(xb, w_qkv_b, modes_b, w_qkv_b.shape[1], BF16, _rope_tables(seq_len, HEAD_DIM_B), seq_len)

    q_c = _proj(xb, cols(o_qc, WIDTH_C), ((("plain", HEAD_DIM_C ** -0.5),) * (WIDTH_C // LANES),),
                WIDTH_C, BF16)
    gate_tn = 1536
    gates = _proj(xb, cols(o_gate, 3 * d_model),
                  ((("sigmoid", 1.0),) * (gate_tn // LANES),) * (3 * d_model // gate_tn), gate_tn, BF16)
    mkv = _proj(mem.reshape(bsz * n_mem, d_model).astype(BF16), w_mem_kv.astype(BF16),
                ((("plain", 1.0),) * (2 * WIDTH_C // LANES),), 2 * WIDTH_C, BF16)

    tq = min(512, seq_len // max(DILATIONS_A))
    oas, lses = [], []
    for d, qkv_g in zip(DILATIONS_A, qkv_a):
        first = (jnp.arange(n_tok // tq, dtype=jnp.int32) % (seq_len // d // tq) == 0).astype(jnp.int32)
        o_g, lse_g = _attn_a(qkv_g.reshape(n_tok, 3 * GROUP_WIDTH_A), first, tq)
        oas.append(o_g.reshape(bsz, d, seq_len // d, GROUP_WIDTH_A))
        lses.append(lse_g.reshape(bsz, d, seq_len // d, GROUP_WIDTH_A))

    yb = _attn_b(qkv_b, sinks.astype(F32), seq_len, tq)
    yc = _attn_c(q_c, mkv, seq_len, n_mem, tq)

    h1, h1_slab = _merge(oas, lses, yb, yc, gates, x2, w_a.astype(BF16), w_b.astype(BF16), w_c.astype(BF16),
                         w_out.astype(BF16), ln1_g.reshape(1, d_model), ln1_b.reshape(1, d_model), seq_len,
                         tm=256)

    rw_t = router_w.T
    rw_hi = rw_t.astype(BF16)
    rw_lo = (rw_t - rw_hi.astype(F32)).astype(BF16)
    top_idx, top_w, cnt = _router(h1, rw_hi, rw_lo, router_bias.reshape(n_experts, 1).astype(F32), tm=256)
    counts = jnp.sum(cnt, axis=1).astype(jnp.int32)
    first_row = (jnp.cumsum(counts) - counts).astype(F32).reshape(n_experts, 1)
    dest = _ranks(top_idx, first_row, tm=256)
    routed = _experts(counts, dest, h1_slab, w_eg, w_eu, w_ed)
    w_sgu = jnp.concatenate([w_sg, w_su], axis=1).astype(BF16)
    out = _final(h1, top_w.T, routed, w_sgu, w_sd.astype(BF16), ln2_g.reshape(1, d_model),
                 ln2_b.reshape(1, d_model), tm=256)
    return out.reshape(bsz, seq_len, d_model)


def kernel(x, mem, w_in, w_mem_kv, attn_sinks, w_branch_a, w_branch_b, w_branch_c, w_out, ln1_g, ln1_b,
           router_w, router_bias, w_exp_gate, w_exp_up, w_exp_down, w_sh_gate, w_sh_up, w_sh_down,
           ln2_g, ln2_b):
    h = x
    for layer in range(DEPTH):
        h = _layer(h, mem, w_in[layer], w_mem_kv[layer], attn_sinks[layer], w_branch_a[layer],
                   w_branch_b[layer], w_branch_c[layer], w_out[layer], ln1_g[layer], ln1_b[layer],
                   router_w[layer], router_bias[layer], w_exp_gate[layer], w_exp_up[layer],
                   w_exp_down[layer], w_sh_gate[layer], w_sh_up[layer], w_sh_down[layer],
                   ln2_g[layer], ln2_b[layer])
    return h
```

```python
import functools

import jax
import jax.numpy as jnp
from jax import lax
from jax.experimental import pallas as pl
from jax.experimental.pallas import tpu as pltpu

F32 = jnp.float32
BF16 = jnp.bfloat16
U32 = jnp.uint32

DEPTH = 1
HEAD_DIM_A = 128
DILATIONS_A = (1, 4, 16)
N_BACK_A = 128
HEADS_PER_GROUP_A = 4
GROUP_WIDTH_A = HEADS_PER_GROUP_A * HEAD_DIM_A
WIDTH_A = len(DILATIONS_A) * GROUP_WIDTH_A
HEAD_DIM_B = 64
N_Q_HEADS_B = 16
N_KV_HEADS_B = 2
WIDTH_QB = N_Q_HEADS_B * HEAD_DIM_B
WIDTH_KVB = N_KV_HEADS_B * HEAD_DIM_B
N_BACK_B = 127
N_HEADS_C = 4
HEAD_DIM_C = 256
WIDTH_C = N_HEADS_C * HEAD_DIM_C
ROPE_THETA = 10000.0
TOP_K = 8
N_EXPERT_GROUPS = 8
TOPK_EXPERT_GROUPS = 4
ROUTED_SCALE = 2.5
ALPHA = (2 * DEPTH) ** 0.25
LN_EPS = 1e-5

LANES = 128
SLAB_COLS = 2 * LANES
ATT_BLK = 128
MOE_ROWS = 128
MOE_BUFS = 4
MOE_AHEAD = 2
TAB_RING = 8
VMEM_LIMIT = 56 * 1024 * 1024
NEG = -1e30


def _cparams(sem):
    return pltpu.CompilerParams(dimension_semantics=sem, vmem_limit_bytes=VMEM_LIMIT)


def _dot(a, b):
    return jnp.dot(a, b, preferred_element_type=F32)


def _dot_nt(a, b):
    return lax.dot_general(a, b, (((1,), (1,)), ((), ())), preferred_element_type=F32)


def _rope_chunk(a, cos, sin, head_dim):
    if head_dim == 128:
        partner = pltpu.roll(a, 64, 1)
    else:
        lane = lax.broadcasted_iota(jnp.int32, a.shape, 1)
        partner = jnp.where((lane % 64) < 32, pltpu.roll(a, 96, 1), pltpu.roll(a, 32, 1))
    return a * cos + partner * sin


def _proj_kernel(*refs, modes, has_tables, dilations):
    n_out = 1 if dilations is None else len(dilations)
    x_ref, w_ref = refs[:2]
    cos_ref, sin_ref = refs[2:4] if has_tables else (None, None)
    o_refs = refs[2 + 2 * has_tables:2 + 2 * has_tables + n_out]
    stage = refs[-1] if dilations is not None else None
    j = pl.program_id(0)
    acc = _dot(x_ref[...], w_ref[...])
    tm = acc.shape[0]

    def store(c, a):
        if dilations is None:
            o_refs[0][:, c * LANES:(c + 1) * LANES] = a.astype(o_refs[0].dtype)
            return
        per_group = len(modes[0]) // n_out
        gi, cg = divmod(c, per_group)
        d, o_ref = dilations[gi], o_refs[gi]
        lanes = slice(cg * LANES, (cg + 1) * LANES)
        if d == 1:
            o_ref[0, 0, :, lanes] = a.astype(o_ref.dtype)
        else:
            stage[c] = a
            for r in range(d):
                o_ref[0, r, :, lanes] = stage[c, pl.ds(r, tm // d, stride=d), :].astype(o_ref.dtype)

    def epilogue(chunk_modes):
        for c, (kind, scale) in enumerate(chunk_modes):
            a = acc[:, c * LANES:(c + 1) * LANES]
            if kind == "rope128":
                a = _rope_chunk(a, cos_ref[...], sin_ref[...], 128)
            elif kind == "rope64":
                a = _rope_chunk(a, cos_ref[...], sin_ref[...], 64)
            elif kind == "sigmoid":
                a = jax.nn.sigmoid(a)
            if scale != 1.0:
                a = a * scale
            store(c, a)

    if all(m == modes[0] for m in modes):
        epilogue(modes[0])
    else:
        for jj, chunk_modes in enumerate(modes):
            pl.when(j == jj)(functools.partial(epilogue, chunk_modes))


def _proj(x, w, modes, tn, out_dtype, tables=None, seq_len=None, tm=1024, dilations=None):
    m, k = x.shape
    n = w.shape[1]
    tm = min(tm, m)
    assert m % tm == 0 and n % tn == 0 and len(modes) == n // tn
    in_specs = [pl.BlockSpec((tm, k), lambda j, i: (i, 0)),
                pl.BlockSpec((k, tn), lambda j, i: (0, j))]
    args = [x, w]
    if tables is not None:
        per_seq = seq_len // tm
        tab_spec = pl.BlockSpec((tm, LANES), lambda j, i: (i % per_seq, 0))
        in_specs += [tab_spec, tab_spec]
        args += list(tables)
    scratch = []
    if dilations is None:
        out_specs = pl.BlockSpec((tm, tn), lambda j, i: (i, j))
        out_shape = jax.ShapeDtypeStruct((m, n), out_dtype)
    else:
        per_seq = seq_len // tm
        gw = tn // len(dilations)
        out_specs = [pl.BlockSpec((1, d, tm // d, gw), lambda j, i: (i // per_seq, 0, i % per_seq, j))
                     for d in dilations]
        out_shape = [jax.ShapeDtypeStruct((m // seq_len, d, seq_len // d, (n // tn) * gw), out_dtype)
                     for d in dilations]
        scratch = [pltpu.VMEM((tn // LANES, tm, LANES), F32)]
    return pl.pallas_call(
        functools.partial(_proj_kernel, modes=modes, has_tables=tables is not None, dilations=dilations),
        grid=(n // tn, m // tm),
        in_specs=in_specs,
        out_specs=out_specs,
        out_shape=out_shape,
        scratch_shapes=scratch,
        compiler_params=_cparams(("arbitrary", "arbitrary")),
        name="proj",
    )(*args)


def _rope_tables(seq_len, head_dim):
    half = head_dim // 2
    inv_freq = jnp.power(ROPE_THETA, -jnp.arange(half, dtype=F32) * 2.0 / head_dim)
    ang = jnp.arange(seq_len).astype(F32)[:, None] * inv_freq[None, :]
    cos, sin = jnp.cos(ang), jnp.sin(ang)
    reps = LANES // head_dim
    cos_t = jnp.tile(jnp.concatenate([cos, cos], axis=-1), (1, reps))
    sin_t = jnp.tile(jnp.concatenate([-sin, sin], axis=-1), (1, reps))
    return cos_t, sin_t


def _softmax_rows(s):
    m = jnp.max(s, axis=-1, keepdims=True)
    p = jnp.exp(s - m)
    l = jnp.sum(p, axis=-1, keepdims=True)
    return m, p, l


def _attn_a_kernel(first_ref, q_ref, kc_ref, vc_ref, kp_ref, vp_ref, o_ref, lse_ref, *, tq):
    c = pl.program_id(0)
    not_first = first_ref[c] == 0
    row = lax.broadcasted_iota(jnp.int32, (ATT_BLK, 2 * ATT_BLK), 0)
    col = lax.broadcasted_iota(jnp.int32, (ATT_BLK, 2 * ATT_BLK), 1)
    band = (col >= row) & (col <= row + N_BACK_A)
    band_first = band & ((col >= ATT_BLK) | not_first)
    for h in range(HEADS_PER_GROUP_A):
        hs = slice(h * HEAD_DIM_A, (h + 1) * HEAD_DIM_A)
        for i in range(tq // ATT_BLK):
            rows = slice(i * ATT_BLK, (i + 1) * ATT_BLK)
            q = q_ref[rows, hs]
            if i == 0:
                k = jnp.concatenate([kp_ref[:, hs], kc_ref[rows, hs]], axis=0)
                v = jnp.concatenate([vp_ref[:, hs], vc_ref[rows, hs]], axis=0)
                mask = band_first
            else:
                kv_rows = slice((i - 1) * ATT_BLK, (i + 1) * ATT_BLK)
                k = kc_ref[kv_rows, hs]
                v = vc_ref[kv_rows, hs]
                mask = band
            s = jnp.where(mask, _dot_nt(q, k), NEG)
            m, p, l = _softmax_rows(s)
            o = _dot(p.astype(BF16), v) / l
            o_ref[rows, hs] = o.astype(o_ref.dtype)
            lse_ref[rows, hs] = jnp.broadcast_to(m + jnp.log(l), (ATT_BLK, HEAD_DIM_A))


def _attn_a(qkv, first, tq):
    r = qkv.shape[0]
    gw = GROUP_WIDTH_A
    sub = tq // ATT_BLK
    cur = lambda col: pl.BlockSpec((tq, gw), lambda c, f: (c, col))
    prev = lambda col: pl.BlockSpec((ATT_BLK, gw), lambda c, f: (jnp.maximum(c * sub - 1, 0), col))
    return pl.pallas_call(
        functools.partial(_attn_a_kernel, tq=tq),
        grid_spec=pltpu.PrefetchScalarGridSpec(
            num_scalar_prefetch=1,
            grid=(r // tq,),
            in_specs=[cur(0), cur(1), cur(2), prev(1), prev(2)],
            out_specs=[pl.BlockSpec((tq, gw), lambda c, f: (c, 0)),
                       pl.BlockSpec((tq, gw), lambda c, f: (c, 0))],
        ),
        out_shape=[jax.ShapeDtypeStruct((r, gw), BF16), jax.ShapeDtypeStruct((r, gw), F32)],
        compiler_params=_cparams(("arbitrary",)),
        name="attn_a",
    )(first, qkv, qkv, qkv, qkv, qkv)


def _attn_b_kernel(sink_ref, q_ref, kvc_ref, kvp_ref, o_ref, *, tq, chunks_per_seq):
    c = pl.program_id(0)
    not_first = (c % chunks_per_seq) != 0
    row = lax.broadcasted_iota(jnp.int32, (ATT_BLK, 2 * ATT_BLK), 0)
    col = lax.broadcasted_iota(jnp.int32, (ATT_BLK, 2 * ATT_BLK), 1)
    band = (col >= row + 1) & (col <= row + 1 + N_BACK_B)
    band_first = band & ((col >= ATT_BLK) | not_first)
    lane_kv = lax.broadcasted_iota(jnp.int32, (2 * ATT_BLK, LANES), 1) < HEAD_DIM_B
    lane_o = lax.broadcasted_iota(jnp.int32, (ATT_BLK, LANES), 1) < HEAD_DIM_B
    pairs = (N_Q_HEADS_B // N_KV_HEADS_B) // 2
    for i in range(tq // ATT_BLK):
        rows = slice(i * ATT_BLK, (i + 1) * ATT_BLK)
        if i == 0:
            kv = jnp.concatenate([kvp_ref[...], kvc_ref[rows, :]], axis=0)
            mask = band_first
        else:
            kv = kvc_ref[(i - 1) * ATT_BLK:(i + 1) * ATT_BLK, :]
            mask = band
        for n in range(N_KV_HEADS_B):
            kk = kv[:, n * LANES:(n + 1) * LANES]
            vv = kv[:, (N_KV_HEADS_B + n) * LANES:(N_KV_HEADS_B + n + 1) * LANES]
            zero = jnp.zeros_like(kk)
            k_bd = jnp.concatenate([jnp.where(lane_kv, kk, zero), jnp.where(lane_kv, zero, kk)], axis=0)
            v_bd = jnp.concatenate([jnp.where(lane_kv, vv, zero), jnp.where(lane_kv, zero, vv)], axis=0)
            for jp in range(pairs):
                c0 = n * (WIDTH_QB // N_KV_HEADS_B) + jp * LANES
                q = q_ref[rows, c0:c0 + LANES]
                s2 = _dot_nt(q, k_bd)
                ps, invs = [], []
                for hh in range(2):
                    sink = sink_ref[n * 2 * pairs + jp * 2 + hh]
                    s = jnp.where(mask, s2[:, hh * 2 * ATT_BLK:(hh + 1) * 2 * ATT_BLK], NEG)
                    m = jnp.maximum(jnp.max(s, axis=-1, keepdims=True), sink)
                    p = jnp.exp(s - m)
                    denom = jnp.sum(p, axis=-1, keepdims=True) + jnp.exp(sink - m)
                    ps.append(p.astype(BF16))
                    invs.append(1.0 / denom)
                o = _dot(jnp.concatenate(ps, axis=1), v_bd)
                o = o * jnp.where(lane_o, invs[0], invs[1])
                o_ref[rows, c0:c0 + LANES] = o.astype(o_ref.dtype)


def _attn_b(qkv, sinks, seq_len, tq):
    t = qkv.shape[0]
    sub = tq // ATT_BLK
    kv_col = WIDTH_QB // (4 * LANES)
    return pl.pallas_call(
        functools.partial(_attn_b_kernel, tq=tq, chunks_per_seq=seq_len // tq),
        grid=(t // tq,),
        in_specs=[pl.BlockSpec(memory_space=pltpu.SMEM),
                  pl.BlockSpec((tq, WIDTH_QB), lambda c: (c, 0)),
                  pl.BlockSpec((tq, 4 * LANES), lambda c: (c, kv_col)),
                  pl.BlockSpec((ATT_BLK, 4 * LANES), lambda c: (jnp.maximum(c * sub - 1, 0), kv_col))],
        out_specs=pl.BlockSpec((tq, WIDTH_QB), lambda c: (c, 0)),
        out_shape=jax.ShapeDtypeStruct((t, WIDTH_QB), BF16),
        compiler_params=_cparams(("arbitrary",)),
        name="attn_b",
    )(sinks, qkv, qkv, qkv)


def _attn_c_kernel(q_ref, mkv_ref, o_ref, *, tq, sub):
    for h in range(N_HEADS_C):
        hs = slice(h * HEAD_DIM_C, (h + 1) * HEAD_DIM_C)
        mk = mkv_ref[:, hs]
        mv = mkv_ref[:, WIDTH_C + h * HEAD_DIM_C:WIDTH_C + (h + 1) * HEAD_DIM_C]
        for i in range(tq // sub):
            rows = slice(i * sub, (i + 1) * sub)
            s = _dot_nt(q_ref[rows, hs], mk)
            _, p, l = _softmax_rows(s)
            o_ref[rows, hs] = (_dot(p.astype(BF16), mv) / l).astype(o_ref.dtype)


def _attn_c(qc, mkv, seq_len, n_mem, tq):
    t = qc.shape[0]
    per_seq = seq_len // tq
    return pl.pallas_call(
        functools.partial(_attn_c_kernel, tq=tq, sub=min(tq, 256)),
        grid=(t // tq,),
        in_specs=[pl.BlockSpec((tq, WIDTH_C), lambda c: (c, 0)),
                  pl.BlockSpec((n_mem, 2 * WIDTH_C), lambda c: (c // per_seq, 0))],
        out_specs=pl.BlockSpec((tq, WIDTH_C), lambda c: (c, 0)),
        out_shape=jax.ShapeDtypeStruct((t, WIDTH_C), BF16),
        compiler_params=_cparams(("arbitrary",)),
        name="attn_c",
    )(qc, mkv)


def _slab_store(ref, val):
    rows, width = val.shape
    n = width // (2 * LANES)
    for p in range(n):
        lo = lax.bitcast_convert_type(val[:, (2 * p) * LANES:(2 * p + 1) * LANES].astype(BF16).astype(F32), U32)
        hi = lax.bitcast_convert_type(val[:, (2 * p + 1) * LANES:(2 * p + 2) * LANES].astype(BF16).astype(F32), U32)
        ref[pl.ds(p, rows, stride=n), :] = (lo >> 16) | (hi & jnp.uint32(0xFFFF0000))


def _slab_load(ref, rows, width):
    n = width // (2 * LANES)
    chunks = []
    for p in range(n):
        w = ref[pl.ds(p, rows, stride=n), :]
        chunks.append(lax.bitcast_convert_type(w << 16, F32))
        chunks.append(lax.bitcast_convert_type(w & jnp.uint32(0xFFFF0000), F32))
    return jnp.concatenate(chunks, axis=1)


def _layer_norm(z, g, b):
    mu = jnp.mean(z, axis=-1, keepdims=True)
    zc = z - mu
    var = jnp.mean(zc * zc, axis=-1, keepdims=True)
    return zc * lax.rsqrt(var + LN_EPS) * g + b


def _merge_kernel(*refs, d_model):
    n_grp = len(DILATIONS_A)
    oa_refs, lse_refs = refs[:n_grp], refs[n_grp:2 * n_grp]
    (yb_ref, yc_ref, gate_ref, x_ref, wa_ref, wb_ref, wc_ref, wo_ref, g_ref, b_ref,
     h_ref, hs_ref, oa_s, lse_s) = refs[2 * n_grp:]
    tm = x_ref.shape[0]

    def natural(ref, stage, g):
        d = DILATIONS_A[g]
        if d == 1:
            return ref[0, 0].astype(F32)
        chunks = GROUP_WIDTH_A // LANES
        for r in range(d):
            for c in range(chunks):
                stage[g, c, pl.ds(r, tm // d, stride=d), :] = ref[0, r, :, c * LANES:(c + 1) * LANES].astype(F32)
        return jnp.concatenate([stage[g, c] for c in range(chunks)], axis=1)

    lses = [natural(lse_refs[g], lse_s, g) for g in range(n_grp)]
    top = functools.reduce(jnp.maximum, lses)
    es = [jnp.exp(l - top) for l in lses]
    tot = functools.reduce(jnp.add, es)
    ya = functools.reduce(jnp.add, [e * natural(oa_refs[g], oa_s, g) for g, e in enumerate(es)]) / tot
    gate = lambda g: gate_ref[:, g * d_model:(g + 1) * d_model].astype(F32)
    merged = gate(0) * _dot(ya.astype(BF16), wa_ref[...])
    merged += gate(1) * _dot(yb_ref[...], wb_ref[...])
    merged += gate(2) * _dot(yc_ref[...], wc_ref[...])
    mix = _dot(merged.astype(BF16), wo_ref[...])
    h = _layer_norm(ALPHA * x_ref[...] + mix, g_ref[...], b_ref[...])
    h_ref[...] = h
    _slab_store(hs_ref, h)


def _merge(oas, lses, yb, yc, gates, x2, wa, wb, wc, wo, ln_g, ln_b, seq_len, tm):
    t, d_model = x2.shape
    per_seq = seq_len // tm
    row = lambda w: pl.BlockSpec((tm, w), lambda i: (i, 0))
    grp = lambda d: pl.BlockSpec((1, d, tm // d, GROUP_WIDTH_A), lambda i: (i // per_seq, 0, i % per_seq, 0))
    const = lambda a: pl.BlockSpec(a.shape, lambda i: (0, 0), pipeline_mode=pl.Buffered(1))
    n_grp = len(DILATIONS_A)
    return pl.pallas_call(
        functools.partial(_merge_kernel, d_model=d_model),
        grid=(t // tm,),
        in_specs=[grp(d) for d in DILATIONS_A] * 2 + [row(WIDTH_QB), row(WIDTH_C),
                  row(3 * d_model), row(d_model), const(wa), const(wb), const(wc), const(wo),
                  const(ln_g), const(ln_b)],
        out_specs=[row(d_model), pl.BlockSpec((tm * (d_model // SLAB_COLS), LANES), lambda i: (i, 0))],
        out_shape=[jax.ShapeDtypeStruct((t, d_model), F32),
                   jax.ShapeDtypeStruct((t * (d_model // SLAB_COLS), LANES), U32)],
        scratch_shapes=[pltpu.VMEM((n_grp, GROUP_WIDTH_A // LANES, tm, LANES), F32)] * 2,
        compiler_params=_cparams(("arbitrary",)),
        name="merge",
    )(*oas, *lses, yb, yc, gates, x2, wa, wb, wc, wo, ln_g, ln_b)


def _router_kernel(h_ref, whi_ref, wlo_ref, bias_ref, idx_ref, wgt_ref, cnt_ref, *, n_experts):
    h = h_ref[...]
    h_hi = h.astype(BF16)
    h_lo = (h - h_hi.astype(F32)).astype(BF16)
    logits = _dot_nt(whi_ref[...], h_hi) + _dot_nt(whi_ref[...], h_lo) + _dot_nt(wlo_ref[...], h_hi)
    scores = jax.nn.sigmoid(logits)
    biased = scores + bias_ref[...]
    tm = scores.shape[1]
    per_group = n_experts // N_EXPERT_GROUPS
    neg_inf = -jnp.inf

    b3 = biased.reshape(N_EXPERT_GROUPS, per_group, tm)
    i3 = lax.broadcasted_iota(jnp.int32, b3.shape, 1)
    m1 = jnp.max(b3, axis=1, keepdims=True)
    a1 = jnp.min(jnp.where(b3 == m1, i3, per_group), axis=1, keepdims=True)
    m2 = jnp.max(jnp.where(i3 == a1, neg_inf, b3), axis=1, keepdims=True)
    gscore = (m1 + m2).reshape(N_EXPERT_GROUPS, tm)

    gi = lax.broadcasted_iota(jnp.int32, gscore.shape, 0)
    chosen = jnp.zeros(gscore.shape, jnp.int32)
    for _ in range(TOPK_EXPERT_GROUPS):
        gm = jnp.max(gscore, axis=0, keepdims=True)
        ga = jnp.min(jnp.where(gscore == gm, gi, N_EXPERT_GROUPS), axis=0, keepdims=True)
        hit = gi == ga
        chosen = jnp.where(hit, 1, chosen)
        gscore = jnp.where(hit, neg_inf, gscore)

    cur = jnp.where(chosen.reshape(N_EXPERT_GROUPS, 1, tm) > 0, b3, neg_inf).reshape(n_experts, tm)
    ei = lax.broadcasted_iota(jnp.int32, cur.shape, 0)
    idxs, wgts = [], []
    member = jnp.zeros(cur.shape, F32)
    for _ in range(TOP_K):
        m = jnp.max(cur, axis=0, keepdims=True)
        a = jnp.min(jnp.where(cur == m, ei, n_experts), axis=0, keepdims=True)
        hit = ei == a
        idxs.append(a)
        wgts.append(jnp.sum(jnp.where(hit, scores, 0.0), axis=0, keepdims=True))
        member = jnp.where(hit, 1.0, member)
        cur = jnp.where(hit, neg_inf, cur)
    wsum = functools.reduce(jnp.add, wgts)
    idx_ref[...] = jnp.concatenate(idxs, axis=0)
    wgt_ref[...] = jnp.concatenate(wgts, axis=0) / wsum * ROUTED_SCALE

    @pl.when(pl.program_id(0) == 0)
    def _():
        cnt_ref[...] = jnp.zeros_like(cnt_ref)

    cnt_ref[...] += functools.reduce(
        jnp.add, [member[:, c * LANES:(c + 1) * LANES] for c in range(tm // LANES)])


def _router(h1, w_hi, w_lo, bias, tm):
    t, d_model = h1.shape
    n_experts = w_hi.shape[0]
    const = lambda a: pl.BlockSpec(a.shape, lambda i: (0, 0))
    return pl.pallas_call(
        functools.partial(_router_kernel, n_experts=n_experts),
        grid=(t // tm,),
        in_specs=[pl.BlockSpec((tm, d_model), lambda i: (i, 0)), const(w_hi), const(w_lo), const(bias)],
        out_specs=[pl.BlockSpec((TOP_K, tm), lambda i: (0, i)),
                   pl.BlockSpec((TOP_K, tm), lambda i: (0, i)),
                   pl.BlockSpec((n_experts, LANES), lambda i: (0, 0))],
        out_shape=[jax.ShapeDtypeStruct((TOP_K, t), jnp.int32), jax.ShapeDtypeStruct((TOP_K, t), F32),
                   jax.ShapeDtypeStruct((n_experts, LANES), F32)],
        compiler_params=_cparams(("arbitrary",)),
        name="router",
    )(h1, w_hi, w_lo, bias)


def _rank_kernel(idx_ref, base_ref, dest_ref, carry, *, n_experts):
    @pl.when(pl.program_id(0) == 0)
    def _():
        carry[...] = jnp.zeros_like(carry)

    idx = idx_ref[...]
    tm = idx.shape[1]
    ei = lax.broadcasted_iota(jnp.int32, (n_experts, tm), 0)
    hits = [ei == idx[k:k + 1, :] for k in range(TOP_K)]
    member = functools.reduce(jnp.add, [jnp.where(h, 1.0, 0.0) for h in hits])
    earlier = (lax.broadcasted_iota(jnp.int32, (tm, tm), 0)
               < lax.broadcasted_iota(jnp.int32, (tm, tm), 1))
    before = _dot(member.astype(BF16), jnp.where(earlier, 1.0, 0.0).astype(BF16))
    row = base_ref[...] + carry[...] + before
    dest = [jnp.sum(jnp.where(h, row, 0.0), axis=0, keepdims=True) for h in hits]
    dest_ref[...] = jnp.concatenate(dest, axis=0).astype(jnp.int32)
    carry[...] += jnp.sum(member, axis=1, keepdims=True)


def _ranks(top_idx, base, tm):
    k, t = top_idx.shape
    n_experts = base.shape[0]
    return pl.pallas_call(
        functools.partial(_rank_kernel, n_experts=n_experts),
        grid=(t // tm,),
        in_specs=[pl.BlockSpec((k, tm), lambda i: (0, i)), pl.BlockSpec((n_experts, 1), lambda i: (0, 0))],
        out_specs=pl.BlockSpec((k, tm), lambda i: (0, i)),
        out_shape=jax.ShapeDtypeStruct((k, t), jnp.int32),
        scratch_shapes=[pltpu.VMEM((n_experts, 1), F32)],
        compiler_params=_cparams(("arbitrary",)),
        name="ranks",
    )(top_idx, base)


def _expert_kernel(blk0_ref, nblk_ref, tab_hbm, h_hbm, wg_ref, wu_ref, wd_ref, out_hbm,
                   tab, wgu_s, wd_s, xb_s, xbuf, ybuf, gsem, ssem, tsem, *, n_tok, d_expert):
    n_real_rows = TOP_K * n_tok
    d_model = xb_s.shape[1]
    sl = d_model // SLAB_COLS
    e = pl.program_id(0)
    nb = nblk_ref[e]
    b0 = blk0_ref[e]

    def tab_copy(g):
        slot = (g + 1) & (TAB_RING - 1)
        return pltpu.make_async_copy(tab_hbm.at[pl.ds(g + 1, 1), :], tab.at[pl.ds(slot, 1), :], tsem.at[slot])

    def tab_start(g):
        tab_copy(g).start()

    def tab_wait(g):
        tab_copy(g).wait()

    def slab(ref, row):
        return ref.at[pl.ds(pl.multiple_of(row * sl, sl), sl), :]

    def start_gather(g):
        buf = g & (MOE_BUFS - 1)
        trow = (g + 1) & (TAB_RING - 1)
        for r in range(MOE_ROWS):
            tok = tab[trow, r] & (n_tok - 1)
            pltpu.make_async_copy(slab(h_hbm, tok), slab(xbuf.at[buf], r), gsem.at[buf]).start(priority=1)

    def start_scatter(g):
        buf = g & (MOE_BUFS - 1)
        trow = (g + 1) & (TAB_RING - 1)
        for r in range(MOE_ROWS):
            pltpu.make_async_copy(slab(ybuf.at[buf], r), slab(out_hbm, tab[trow, r]),
                                  ssem.at[buf]).start(priority=1)

    def wait_gather(g):
        buf = g & (MOE_BUFS - 1)
        pltpu.make_async_copy(h_hbm.at[pl.ds(0, MOE_ROWS * sl), :], xbuf.at[buf], gsem.at[buf]).wait()

    def wait_scatter(g):
        buf = g & (MOE_BUFS - 1)
        pltpu.make_async_copy(ybuf.at[buf], out_hbm.at[pl.ds(0, MOE_ROWS * sl), :], ssem.at[buf]).wait()

    @pl.when(e == 0)
    def _():
        for g in range(-1, MOE_AHEAD + 1):
            tab_start(g)
        ybuf[...] = jnp.zeros_like(ybuf)
        for g in range(-1, MOE_AHEAD):
            tab_wait(g)
        for buf in range(MOE_BUFS - 1):
            for r in range(MOE_ROWS):
                pltpu.make_async_copy(slab(ybuf.at[buf], r), slab(out_hbm, n_real_rows + buf * MOE_ROWS + r),
                                      ssem.at[buf]).start(priority=1)
        for g in range(MOE_AHEAD):
            start_gather(g)

    @pl.when(nb > 0)
    def _():
        wgu_s[:, :d_expert] = wg_ref[0].astype(BF16)
        wgu_s[:, d_expert:] = wu_ref[0].astype(BF16)
        wd_s[...] = wd_ref[0].astype(BF16)

        def block(b, _):
            g = b0 + b
            tab_start(g + MOE_AHEAD + 1)
            tab_wait(g + MOE_AHEAD)
            wait_gather(g)
            wait_scatter(g - MOE_BUFS)
            xb_s[...] = _slab_load(xbuf.at[g & (MOE_BUFS - 1)], MOE_ROWS, d_model).astype(BF16)
            start_gather(g + MOE_AHEAD)
            start_scatter(g - 1)
            gu = _dot(xb_s[...], wgu_s[...])
            act = gu[:, :d_expert]
            hid = (act * jax.nn.sigmoid(act) * gu[:, d_expert:]).astype(BF16)
            _slab_store(ybuf.at[g & (MOE_BUFS - 1)], _dot(hid, wd_s[...]))
            return 0

        lax.fori_loop(0, nb, block, 0)

    @pl.when(e == pl.num_programs(0) - 1)
    def _():
        n_act = b0 + nb
        start_scatter(n_act - 1)
        tab_wait(n_act + MOE_AHEAD)
        for i in range(MOE_BUFS):
            wait_scatter(n_act - 1 - i)
        for i in range(MOE_AHEAD):
            wait_gather(n_act + i)


def _slot_table_kernel(c0_ref, nv_ref, order_ref, tab_ref, *, n_real_rows):
    lane = lax.broadcasted_iota(jnp.int32, (1, LANES), 1)
    rows = []
    for j in range(tab_ref.shape[0]):
        gx = pl.program_id(0) * tab_ref.shape[0] + j
        c, n_valid = c0_ref[gx], nv_ref[gx]
        q, off = c >> 7, c & (LANES - 1)
        shift = (LANES - off) & (LANES - 1)
        a = pltpu.roll(order_ref[pl.ds(q, 1), :], shift, 1)
        b = pltpu.roll(order_ref[pl.ds(q + 1, 1), :], shift, 1)
        slots = jnp.where(lane < LANES - off, a, b)
        scratch = n_real_rows + ((gx - 1) & (MOE_BUFS - 1)) * MOE_ROWS + lane
        rows.append(jnp.where(lane < n_valid, slots, scratch))
    tab_ref[...] = jnp.concatenate(rows, axis=0)


def _slot_table(order, c0, nv, n_real_rows):
    n_tab = c0.shape[0]
    rows_per_step = 8
    return pl.pallas_call(
        functools.partial(_slot_table_kernel, n_real_rows=n_real_rows),
        grid_spec=pltpu.PrefetchScalarGridSpec(
            num_scalar_prefetch=2,
            grid=(n_tab // rows_per_step,),
            in_specs=[pl.BlockSpec(order.shape, lambda i, *_: (0, 0))],
            out_specs=pl.BlockSpec((rows_per_step, LANES), lambda i, *_: (i, 0)),
        ),
        out_shape=jax.ShapeDtypeStruct((n_tab, LANES), jnp.int32),
        compiler_params=_cparams(("arbitrary",)),
        name="slot_table",
    )(c0, nv, order)


def _experts(counts, dest, h1_slab, w_gate, w_up, w_down):
    n_experts, d_model, d_expert = w_gate.shape
    sl = d_model // SLAB_COLS
    n_tok = h1_slab.shape[0] // sl
    n_assign = TOP_K * n_tok
    assert n_tok & (n_tok - 1) == 0 and MOE_AHEAD + 3 <= TAB_RING and MOE_AHEAD < MOE_BUFS
    assert MOE_ROWS == LANES
    nblk = (counts + MOE_ROWS - 1) // MOE_ROWS
    blk_end = jnp.cumsum(nblk)
    blk0 = blk_end - nblk
    starts = jnp.cumsum(counts) - counts
    _, order = lax.sort_key_val(dest.reshape(n_assign), jnp.arange(n_assign, dtype=jnp.int32))
    order = jnp.concatenate([order, jnp.zeros((2 * LANES,), jnp.int32)]).reshape(n_assign // LANES + 2, LANES)
    n_tab = -(-(n_assign // MOE_ROWS + n_experts + MOE_AHEAD + 2) // 8) * 8
    g = jnp.arange(n_tab, dtype=jnp.int32) - 1
    ge = jnp.minimum(jnp.sum((blk_end[None, :] <= g[:, None]).astype(jnp.int32), axis=1), n_experts - 1)
    onehot = (ge[:, None] == jnp.arange(n_experts, dtype=jnp.int32)[None, :]).astype(jnp.int32)
    pick = lambda v: jnp.sum(onehot * v[None, :], axis=1)
    b = g - pick(blk0)
    live = (g >= 0) & (g < blk_end[-1])
    c0 = jnp.where(live, pick(starts) + b * MOE_ROWS, n_assign).astype(jnp.int32)
    nv = jnp.where(live, jnp.minimum(pick(counts) - b * MOE_ROWS, MOE_ROWS), 0).astype(jnp.int32)
    tab = _slot_table(order, c0, nv, n_assign)
    return pl.pallas_call(
        functools.partial(_expert_kernel, n_tok=n_tok, d_expert=d_expert),
        grid_spec=pltpu.PrefetchScalarGridSpec(
            num_scalar_prefetch=2,
            grid=(n_experts,),
            in_specs=[pl.BlockSpec(memory_space=pl.ANY),
                      pl.BlockSpec(memory_space=pl.ANY),
                      pl.BlockSpec((1, d_model, d_expert), lambda e, *_: (e, 0, 0)),
                      pl.BlockSpec((1, d_model, d_expert), lambda e, *_: (e, 0, 0)),
                      pl.BlockSpec((1, d_expert, d_model), lambda e, *_: (e, 0, 0))],
            out_specs=pl.BlockSpec(memory_space=pl.ANY),
            scratch_shapes=[pltpu.SMEM((TAB_RING, MOE_ROWS), jnp.int32),
                            pltpu.VMEM((d_model, 2 * d_expert), BF16),
                            pltpu.VMEM((d_expert, d_model), BF16),
                            pltpu.VMEM((MOE_ROWS, d_model), BF16),
                            pltpu.VMEM((MOE_BUFS, MOE_ROWS * sl, LANES), U32),
                            pltpu.VMEM((MOE_BUFS, MOE_ROWS * sl, LANES), U32),
                            pltpu.SemaphoreType.DMA((MOE_BUFS,)),
                            pltpu.SemaphoreType.DMA((MOE_BUFS,)),
                            pltpu.SemaphoreType.DMA((TAB_RING,))],
        ),
        out_shape=jax.ShapeDtypeStruct(((n_assign + MOE_BUFS * MOE_ROWS) * sl, LANES), U32),
        compiler_params=_cparams(("arbitrary",)),
        name="experts",
    )(blk0.astype(jnp.int32), nblk.astype(jnp.int32), tab, h1_slab, w_gate, w_up, w_down)


def _final_kernel(*refs, d_expert):
    h_ref, wt_ref = refs[0], refs[1]
    routed_refs = refs[2:2 + TOP_K]
    wgu_ref, wd_ref, g_ref, b_ref, o_ref = refs[2 + TOP_K:]
    h = h_ref[...]
    gu = _dot(h.astype(BF16), wgu_ref[...])
    act = gu[:, :d_expert]
    hid = (act * jax.nn.sigmoid(act) * gu[:, d_expert:]).astype(BF16)
    ff = _dot(hid, wd_ref[...])
    wt = wt_ref[...]
    rows, d_model = h.shape
    for k in range(TOP_K):
        ff += _slab_load(routed_refs[k], rows, d_model) * wt[:, k:k + 1]
    o_ref[...] = _layer_norm(ALPHA * h + ff, g_ref[...], b_ref[...])


def _final(h1, wt, routed, w_gu, w_d, ln_g, ln_b, tm):
    t, d_model = h1.shape
    d_expert = w_d.shape[0]
    per_k = t // tm
    row = lambda w: pl.BlockSpec((tm, w), lambda i: (i, 0))
    const = lambda a: pl.BlockSpec(a.shape, lambda i: (0, 0))
    routed_specs = [pl.BlockSpec((tm * (d_model // SLAB_COLS), LANES), lambda i, k=k: (k * per_k + i, 0))
                    for k in range(TOP_K)]
    return pl.pallas_call(
        functools.partial(_final_kernel, d_expert=d_expert),
        grid=(t // tm,),
        in_specs=[row(d_model), row(TOP_K)] + routed_specs + [const(w_gu), const(w_d), const(ln_g), const(ln_b)],
        out_specs=row(d_model),
        out_shape=jax.ShapeDtypeStruct((t, d_model), F32),
        compiler_params=_cparams(("arbitrary",)),
        name="final",
    )(h1, wt, *([routed] * TOP_K), w_gu, w_d, ln_g, ln_b)


def _layer(h, mem, w_in, w_mem_kv, sinks, w_a, w_b, w_c, w_out, ln1_g, ln1_b, router_w, router_bias,
           w_eg, w_eu, w_ed, w_sg, w_su, w_sd, ln2_g, ln2_b):
    bsz, seq_len, d_model = h.shape
    n_tok = bsz * seq_len
    n_mem = mem.shape[1]
    n_experts = router_w.shape[1]
    x2 = h.reshape(n_tok, d_model)
    xb = x2.astype(BF16)

    o_qa, o_ka, o_va = 0, WIDTH_A, 2 * WIDTH_A
    o_qb = 3 * WIDTH_A
    o_kb = o_qb + WIDTH_QB
    o_vb = o_kb + WIDTH_KVB
    o_qc = o_vb + WIDTH_KVB
    o_gate = o_qc + WIDTH_C
    wb16 = w_in.astype(BF16)
    cols = lambda o, n: wb16[:, o:o + n]
    chunks_a = WIDTH_A // LANES
    scale_a = HEAD_DIM_A ** -0.5
    tabs_a = _rope_tables(seq_len, HEAD_DIM_A)
    qkv_a = _proj(xb, cols(0, 3 * WIDTH_A),
                  ((("rope128", scale_a),) * chunks_a, (("rope128", 1.0),) * chunks_a,
                   (("plain", 1.0),) * chunks_a),
                  WIDTH_A, BF16, tabs_a, seq_len, dilations=DILATIONS_A)

    hb = HEAD_DIM_B
    dup = lambda o: [cols(o + n * hb, hb) for n in range(N_KV_HEADS_B) for _ in range(2)]
    w_qkv_b = jnp.concatenate([cols(o_qb, WIDTH_QB)] + dup(o_kb) + dup(o_vb), axis=1)
    q_chunks = WIDTH_QB // LANES
    modes_b = ((("rope64", HEAD_DIM_B ** -0.5),) * q_chunks + (("rope64", 1.0),) * N_KV_HEADS_B
               + (("plain", 1.0),) * N_KV_HEADS_B,)
    qkv_b = _proj(xb, w_qkv_b, modes_b, w_qkv_b.shape[1], BF16, _rope_tables(seq_len, HEAD_DIM_B), seq_len)

    q_c = _proj(xb, cols(o_qc, WIDTH_C), ((("plain", HEAD_DIM_C ** -0.5),) * (WIDTH_C // LANES),),
                WIDTH_C, BF16)
    gate_tn = 1536
    gates = _proj(xb, cols(o_gate, 3 * d_model),
                  ((("sigmoid", 1.0),) * (gate_tn // LANES),) * (3 * d_model // gate_tn), gate_tn, BF16)
    mkv = _proj(mem.reshape(bsz * n_mem, d_model).astype(BF16), w_mem_kv.astype(BF16),
                ((("plain", 1.0),) * (2 * WIDTH_C // LANES),), 2 * WIDTH_C, BF16)

    tq = min(512, seq_len // max(DILATIONS_A))
    oas, lses = [], []
    for d, qkv_g in zip(DILATIONS_A, qkv_a):
        first = (jnp.arange(n_tok // tq, dtype=jnp.int32) % (seq_len // d // tq) == 0).astype(jnp.int32)
        o_g, lse_g = _attn_a(qkv_g.reshape(n_tok, 3 * GROUP_WIDTH_A), first, tq)
        oas.append(o_g.reshape(bsz, d, seq_len // d, GROUP_WIDTH_A))
        lses.append(lse_g.reshape(bsz, d, seq_len // d, GROUP_WIDTH_A))

    yb = _attn_b(qkv_b, sinks.astype(F32), seq_len, tq)
    yc = _attn_c(q_c, mkv, seq_len, n_mem, tq)

    h1, h1_slab = _merge(oas, lses, yb, yc, gates, x2, w_a.astype(BF16), w_b.astype(BF16), w_c.astype(BF16),
                         w_out.astype(BF16), ln1_g.reshape(1, d_model), ln1_b.reshape(1, d_model), seq_len,
                         tm=256)

    rw_t = router_w.T
    rw_hi = rw_t.astype(BF16)
    rw_lo = (rw_t - rw_hi.astype(F32)).astype(BF16)
    top_idx, top_w, cnt = _router(h1, rw_hi, rw_lo, router_bias.reshape(n_experts, 1).astype(F32), tm=256)
    counts = jnp.sum(cnt, axis=1).astype(jnp.int32)
    first_row = (jnp.cumsum(counts) - counts).astype(F32).reshape(n_experts, 1)
    dest = _ranks(top_idx, first_row, tm=256)
    routed = _experts(counts, dest, h1_slab, w_eg, w_eu, w_ed)
    w_sgu = jnp.concatenate([w_sg, w_su], axis=1).astype(BF16)
    out = _final(h1, top_w.T, routed, w_sgu, w_sd.astype(BF16), ln2_g.reshape(1, d_model),
                 ln2_b.reshape(1, d_model), tm=256)
    return out.reshape(bsz, seq_len, d_model)


def kernel(x, mem, w_in, w_mem_kv, attn_sinks, w_branch_a, w_branch_b, w_branch_c, w_out, ln1_g, ln1_b,
           router_w, router_bias, w_exp_gate, w_exp_up, w_exp_down, w_sh_gate, w_sh_up, w_sh_down,
           ln2_g, ln2_b):
    h = x
    for layer in range(DEPTH):
        h = _layer(h, mem, w_in[layer], w_mem_kv[layer], attn_sinks[layer], w_branch_a[layer],
                   w_branch_b[layer], w_branch_c[layer], w_out[layer], ln1_g[layer], ln1_b[layer],
                   router_w[layer], router_bias[layer], w_exp_gate[layer], w_exp_up[layer],
                   w_exp_down[layer], w_sh_gate[layer], w_sh_up[layer], w_sh_down[layer],
                   ln2_g[layer], ln2_b[layer])
    return h
```

```python
import functools

import jax
import jax.numpy as jnp
from jax import lax
from jax.experimental import pallas as pl
from jax.experimental.pallas import tpu as pltpu

F32 = jnp.float32
BF16 = jnp.bfloat16
U32 = jnp.uint32

DEPTH = 1
HEAD_DIM_A = 128
DILATIONS_A = (1, 4, 16)
N_BACK_A = 128
HEADS_PER_GROUP_A = 4
GROUP_WIDTH_A = HEADS_PER_GROUP_A * HEAD_DIM_A
WIDTH_A = len(DILATIONS_A) * GROUP_WIDTH_A
HEAD_DIM_B = 64
N_Q_HEADS_B = 16
N_KV_HEADS_B = 2
WIDTH_QB = N_Q_HEADS_B * HEAD_DIM_B
WIDTH_KVB = N_KV_HEADS_B * HEAD_DIM_B
N_BACK_B = 127
N_HEADS_C = 4
HEAD_DIM_C = 256
WIDTH_C = N_HEADS_C * HEAD_DIM_C
ROPE_THETA = 10000.0
TOP_K = 8
N_EXPERT_GROUPS = 8
TOPK_EXPERT_GROUPS = 4
ROUTED_SCALE = 2.5
ALPHA = (2 * DEPTH) ** 0.25
LN_EPS = 1e-5

LANES = 128
SLAB_COLS = 2 * LANES
ATT_BLK = 128
MOE_ROWS = 128
MOE_BUFS = 8
MOE_AHEAD = 4
TAB_RING = 8
VMEM_LIMIT = 56 * 1024 * 1024
NEG = -1e30


def _cparams(sem):
    return pltpu.CompilerParams(dimension_semantics=sem, vmem_limit_bytes=VMEM_LIMIT)


def _dot(a, b):
    return jnp.dot(a, b, preferred_element_type=F32)


def _dot_nt(a, b):
    return lax.dot_general(a, b, (((1,), (1,)), ((), ())), preferred_element_type=F32)


def _rope_chunk(a, cos, sin, head_dim):
    if head_dim == 128:
        partner = pltpu.roll(a, 64, 1)
    else:
        lane = lax.broadcasted_iota(jnp.int32, a.shape, 1)
        partner = jnp.where((lane % 64) < 32, pltpu.roll(a, 96, 1), pltpu.roll(a, 32, 1))
    return a * cos + partner * sin


def _proj_kernel(*refs, modes, has_tables, dilations):
    n_out = 1 if dilations is None else len(dilations)
    x_ref, w_ref = refs[:2]
    cos_ref, sin_ref = refs[2:4] if has_tables else (None, None)
    o_refs = refs[2 + 2 * has_tables:2 + 2 * has_tables + n_out]
    stage = refs[-1] if dilations is not None else None
    j = pl.program_id(0)
    acc = _dot(x_ref[...], w_ref[...])
    tm = acc.shape[0]

    def store(c, a):
        if dilations is None:
            o_refs[0][:, c * LANES:(c + 1) * LANES] = a.astype(o_refs[0].dtype)
            return
        per_group = len(modes[0]) // n_out
        gi, cg = divmod(c, per_group)
        d, o_ref = dilations[gi], o_refs[gi]
        lanes = slice(cg * LANES, (cg + 1) * LANES)
        if d == 1:
            o_ref[0, 0, :, lanes] = a.astype(o_ref.dtype)
        else:
            stage[c] = a
            for r in range(d):
                o_ref[0, r, :, lanes] = stage[c, pl.ds(r, tm // d, stride=d), :].astype(o_ref.dtype)

    def epilogue(chunk_modes):
        for c, (kind, scale) in enumerate(chunk_modes):
            a = acc[:, c * LANES:(c + 1) * LANES]
            if kind == "rope128":
                a = _rope_chunk(a, cos_ref[...], sin_ref[...], 128)
            elif kind == "rope64":
                a = _rope_chunk(a, cos_ref[...], sin_ref[...], 64)
            elif kind == "sigmoid":
                a = jax.nn.sigmoid(a)
            if scale != 1.0:
                a = a * scale
            store(c, a)

    if all(m == modes[0] for m in modes):
        epilogue(modes[0])
    else:
        for jj, chunk_modes in enumerate(modes):
            pl.when(j == jj)(functools.partial(epilogue, chunk_modes))


def _proj(x, w, modes, tn, out_dtype, tables=None, seq_len=None, tm=1024, dilations=None):
    m, k = x.shape
    n = w.shape[1]
    tm = min(tm, m)
    assert m % tm == 0 and n % tn == 0 and len(modes) == n // tn
    in_specs = [pl.BlockSpec((tm, k), lambda j, i: (i, 0)),
                pl.BlockSpec((k, tn), lambda j, i: (0, j))]
    args = [x, w]
    if tables is not None:
        per_seq = seq_len // tm
        tab_spec = pl.BlockSpec((tm, LANES), lambda j, i: (i % per_seq, 0))
        in_specs += [tab_spec, tab_spec]
        args += list(tables)
    scratch = []
    if dilations is None:
        out_specs = pl.BlockSpec((tm, tn), lambda j, i: (i, j))
        out_shape = jax.ShapeDtypeStruct((m, n), out_dtype)
    else:
        per_seq = seq_len // tm
        gw = tn // len(dilations)
        out_specs = [pl.BlockSpec((1, d, tm // d, gw), lambda j, i: (i // per_seq, 0, i % per_seq, j))
                     for d in dilations]
        out_shape = [jax.ShapeDtypeStruct((m // seq_len, d, seq_len // d, (n // tn) * gw), out_dtype)
                     for d in dilations]
        scratch = [pltpu.VMEM((tn // LANES, tm, LANES), F32)]
    return pl.pallas_call(
        functools.partial(_proj_kernel, modes=modes, has_tables=tables is not None, dilations=dilations),
        grid=(n // tn, m // tm),
        in_specs=in_specs,
        out_specs=out_specs,
        out_shape=out_shape,
        scratch_shapes=scratch,
        compiler_params=_cparams(("arbitrary", "arbitrary")),
        name="proj",
    )(*args)


def _rope_tables(seq_len, head_dim):
    half = head_dim // 2
    inv_freq = jnp.power(ROPE_THETA, -jnp.arange(half, dtype=F32) * 2.0 / head_dim)
    ang = jnp.arange(seq_len).astype(F32)[:, None] * inv_freq[None, :]
    cos, sin = jnp.cos(ang), jnp.sin(ang)
    reps = LANES // head_dim
    cos_t = jnp.tile(jnp.concatenate([cos, cos], axis=-1), (1, reps))
    sin_t = jnp.tile(jnp.concatenate([-sin, sin], axis=-1), (1, reps))
    return cos_t, sin_t


def _softmax_rows(s):
    m = jnp.max(s, axis=-1, keepdims=True)
    p = jnp.exp(s - m)
    l = jnp.sum(p, axis=-1, keepdims=True)
    return m, p, l


def _attn_a_kernel(first_ref, q_ref, kc_ref, vc_ref, kp_ref, vp_ref, o_ref, lse_ref, *, tq):
    c = pl.program_id(0)
    not_first = first_ref[c] == 0
    row = lax.broadcasted_iota(jnp.int32, (ATT_BLK, 2 * ATT_BLK), 0)
    col = lax.broadcasted_iota(jnp.int32, (ATT_BLK, 2 * ATT_BLK), 1)
    band = (col >= row) & (col <= row + N_BACK_A)
    band_first = band & ((col >= ATT_BLK) | not_first)
    for h in range(HEADS_PER_GROUP_A):
        hs = slice(h * HEAD_DIM_A, (h + 1) * HEAD_DIM_A)
        for i in range(tq // ATT_BLK):
            rows = slice(i * ATT_BLK, (i + 1) * ATT_BLK)
            q = q_ref[rows, hs]
            if i == 0:
                k = jnp.concatenate([kp_ref[:, hs], kc_ref[rows, hs]], axis=0)
                v = jnp.concatenate([vp_ref[:, hs], vc_ref[rows, hs]], axis=0)
                mask = band_first
            else:
                kv_rows = slice((i - 1) * ATT_BLK, (i + 1) * ATT_BLK)
                k = kc_ref[kv_rows, hs]
                v = vc_ref[kv_rows, hs]
                mask = band
            s = jnp.where(mask, _dot_nt(q, k), NEG)
            m, p, l = _softmax_rows(s)
            o = _dot(p.astype(BF16), v) / l
            o_ref[rows, hs] = o.astype(o_ref.dtype)
            lse_ref[rows, hs] = jnp.broadcast_to(m + jnp.log(l), (ATT_BLK, HEAD_DIM_A))


def _attn_a(qkv, first, tq):
    r = qkv.shape[0]
    gw = GROUP_WIDTH_A
    sub = tq // ATT_BLK
    cur = lambda col: pl.BlockSpec((tq, gw), lambda c, f: (c, col))
    prev = lambda col: pl.BlockSpec((ATT_BLK, gw), lambda c, f: (jnp.maximum(c * sub - 1, 0), col))
    return pl.pallas_call(
        functools.partial(_attn_a_kernel, tq=tq),
        grid_spec=pltpu.PrefetchScalarGridSpec(
            num_scalar_prefetch=1,
            grid=(r // tq,),
            in_specs=[cur(0), cur(1), cur(2), prev(1), prev(2)],
            out_specs=[pl.BlockSpec((tq, gw), lambda c, f: (c, 0)),
                       pl.BlockSpec((tq, gw), lambda c, f: (c, 0))],
        ),
        out_shape=[jax.ShapeDtypeStruct((r, gw), BF16), jax.ShapeDtypeStruct((r, gw), F32)],
        compiler_params=_cparams(("arbitrary",)),
        name="attn_a",
    )(first, qkv, qkv, qkv, qkv, qkv)


def _attn_b_kernel(sink_ref, q_ref, kvc_ref, kvp_ref, o_ref, *, tq, chunks_per_seq):
    c = pl.program_id(0)
    not_first = (c % chunks_per_seq) != 0
    row = lax.broadcasted_iota(jnp.int32, (ATT_BLK, 2 * ATT_BLK), 0)
    col = lax.broadcasted_iota(jnp.int32, (ATT_BLK, 2 * ATT_BLK), 1)
    band = (col >= row + 1) & (col <= row + 1 + N_BACK_B)
    band_first = band & ((col >= ATT_BLK) | not_first)
    lane_kv = lax.broadcasted_iota(jnp.int32, (2 * ATT_BLK, LANES), 1) < HEAD_DIM_B
    lane_o = lax.broadcasted_iota(jnp.int32, (ATT_BLK, LANES), 1) < HEAD_DIM_B
    pairs = (N_Q_HEADS_B // N_KV_HEADS_B) // 2
    for i in range(tq // ATT_BLK):
        rows = slice(i * ATT_BLK, (i + 1) * ATT_BLK)
        if i == 0:
            kv = jnp.concatenate([kvp_ref[...], kvc_ref[rows, :]], axis=0)
            mask = band_first
        else:
            kv = kvc_ref[(i - 1) * ATT_BLK:(i + 1) * ATT_BLK, :]
            mask = band
        for n in range(N_KV_HEADS_B):
            kk = kv[:, n * LANES:(n + 1) * LANES]
            vv = kv[:, (N_KV_HEADS_B + n) * LANES:(N_KV_HEADS_B + n + 1) * LANES]
            zero = jnp.zeros_like(kk)
            k_bd = jnp.concatenate([jnp.where(lane_kv, kk, zero), jnp.where(lane_kv, zero, kk)], axis=0)
            v_bd = jnp.concatenate([jnp.where(lane_kv, vv, zero), jnp.where(lane_kv, zero, vv)], axis=0)
            for jp in range(pairs):
                c0 = n * (WIDTH_QB // N_KV_HEADS_B) + jp * LANES
                q = q_ref[rows, c0:c0 + LANES]
                s2 = _dot_nt(q, k_bd)
                ps, invs = [], []
                for hh in range(2):
                    sink = sink_ref[n * 2 * pairs + jp * 2 + hh]
                    s = jnp.where(mask, s2[:, hh * 2 * ATT_BLK:(hh + 1) * 2 * ATT_BLK], NEG)
                    m = jnp.maximum(jnp.max(s, axis=-1, keepdims=True), sink)
                    p = jnp.exp(s - m)
                    denom = jnp.sum(p, axis=-1, keepdims=True) + jnp.exp(sink - m)
                    ps.append(p.astype(BF16))
                    invs.append(1.0 / denom)
                o = _dot(jnp.concatenate(ps, axis=1), v_bd)
                o = o * jnp.where(lane_o, invs[0], invs[1])
                o_ref[rows, c0:c0 + LANES] = o.astype(o_ref.dtype)


def _attn_b(qkv, sinks, seq_len, tq):
    t = qkv.shape[0]
    sub = tq // ATT_BLK
    kv_col = WIDTH_QB // (4 * LANES)
    return pl.pallas_call(
        functools.partial(_attn_b_kernel, tq=tq, chunks_per_seq=seq_len // tq),
        grid=(t // tq,),
        in_specs=[pl.BlockSpec(memory_space=pltpu.SMEM),
                  pl.BlockSpec((tq, WIDTH_QB), lambda c: (c, 0)),
                  pl.BlockSpec((tq, 4 * LANES), lambda c: (c, kv_col)),
                  pl.BlockSpec((ATT_BLK, 4 * LANES), lambda c: (jnp.maximum(c * sub - 1, 0), kv_col))],
        out_specs=pl.BlockSpec((tq, WIDTH_QB), lambda c: (c, 0)),
        out_shape=jax.ShapeDtypeStruct((t, WIDTH_QB), BF16),
        compiler_params=_cparams(("arbitrary",)),
        name="attn_b",
    )(sinks, qkv, qkv, qkv)


def _attn_c_kernel(q_ref, mkv_ref, o_ref, *, tq, sub):
    for h in range(N_HEADS_C):
        hs = slice(h * HEAD_DIM_C, (h + 1) * HEAD_DIM_C)
        mk = mkv_ref[:, hs]
        mv = mkv_ref[:, WIDTH_C + h * HEAD_DIM_C:WIDTH_C + (h + 1) * HEAD_DIM_C]
        for i in range(tq // sub):
            rows = slice(i * sub, (i + 1) * sub)
            s = _dot_nt(q_ref[rows, hs], mk)
            _, p, l = _softmax_rows(s)
            o_ref[rows, hs] = (_dot(p.astype(BF16), mv) / l).astype(o_ref.dtype)


def _attn_c(qc, mkv, seq_len, n_mem, tq):
    t = qc.shape[0]
    per_seq = seq_len // tq
    return pl.pallas_call(
        functools.partial(_attn_c_kernel, tq=tq, sub=min(tq, 256)),
        grid=(t // tq,),
        in_specs=[pl.BlockSpec((tq, WIDTH_C), lambda c: (c, 0)),
                  pl.BlockSpec((n_mem, 2 * WIDTH_C), lambda c: (c // per_seq, 0))],
        out_specs=pl.BlockSpec((tq, WIDTH_C), lambda c: (c, 0)),
        out_shape=jax.ShapeDtypeStruct((t, WIDTH_C), BF16),
        compiler_params=_cparams(("arbitrary",)),
        name="attn_c",
    )(qc, mkv)


def _slab_store(ref, val):
    rows, width = val.shape
    n = width // (2 * LANES)
    for p in range(n):
        lo = lax.bitcast_convert_type(val[:, (2 * p) * LANES:(2 * p + 1) * LANES].astype(BF16).astype(F32), U32)
        hi = lax.bitcast_convert_type(val[:, (2 * p + 1) * LANES:(2 * p + 2) * LANES].astype(BF16).astype(F32), U32)
        ref[pl.ds(p, rows, stride=n), :] = (lo >> 16) | (hi & jnp.uint32(0xFFFF0000))


def _slab_load(ref, rows, width):
    n = width // (2 * LANES)
    chunks = []
    for p in range(n):
        w = ref[pl.ds(p, rows, stride=n), :]
        chunks.append(lax.bitcast_convert_type(w << 16, F32))
        chunks.append(lax.bitcast_convert_type(w & jnp.uint32(0xFFFF0000), F32))
    return jnp.concatenate(chunks, axis=1)


def _layer_norm(z, g, b):
    mu = jnp.mean(z, axis=-1, keepdims=True)
    zc = z - mu
    var = jnp.mean(zc * zc, axis=-1, keepdims=True)
    return zc * lax.rsqrt(var + LN_EPS) * g + b


def _merge_kernel(*refs, d_model):
    n_grp = len(DILATIONS_A)
    oa_refs, lse_refs = refs[:n_grp], refs[n_grp:2 * n_grp]
    (yb_ref, yc_ref, gate_ref, x_ref, wa_ref, wb_ref, wc_ref, wo_ref, g_ref, b_ref,
     h_ref, hs_ref, oa_s, lse_s) = refs[2 * n_grp:]
    tm = x_ref.shape[0]

    def natural(ref, stage, g):
        d = DILATIONS_A[g]
        if d == 1:
            return ref[0, 0].astype(F32)
        chunks = GROUP_WIDTH_A // LANES
        for r in range(d):
            for c in range(chunks):
                stage[g, c, pl.ds(r, tm // d, stride=d), :] = ref[0, r, :, c * LANES:(c + 1) * LANES].astype(F32)
        return jnp.concatenate([stage[g, c] for c in range(chunks)], axis=1)

    lses = [natural(lse_refs[g], lse_s, g) for g in range(n_grp)]
    top = functools.reduce(jnp.maximum, lses)
    es = [jnp.exp(l - top) for l in lses]
    tot = functools.reduce(jnp.add, es)
    ya = functools.reduce(jnp.add, [e * natural(oa_refs[g], oa_s, g) for g, e in enumerate(es)]) / tot
    gate = lambda g: gate_ref[:, g * d_model:(g + 1) * d_model].astype(F32)
    merged = gate(0) * _dot(ya.astype(BF16), wa_ref[...])
    merged += gate(1) * _dot(yb_ref[...], wb_ref[...])
    merged += gate(2) * _dot(yc_ref[...], wc_ref[...])
    mix = _dot(merged.astype(BF16), wo_ref[...])
    h = _layer_norm(ALPHA * x_ref[...] + mix, g_ref[...], b_ref[...])
    h_ref[...] = h
    _slab_store(hs_ref, h)


def _merge(oas, lses, yb, yc, gates, x2, wa, wb, wc, wo, ln_g, ln_b, seq_len, tm):
    t, d_model = x2.shape
    per_seq = seq_len // tm
    row = lambda w: pl.BlockSpec((tm, w), lambda i: (i, 0))
    grp = lambda d: pl.BlockSpec((1, d, tm // d, GROUP_WIDTH_A), lambda i: (i // per_seq, 0, i % per_seq, 0))
    const = lambda a: pl.BlockSpec(a.shape, lambda i: (0, 0), pipeline_mode=pl.Buffered(1))
    n_grp = len(DILATIONS_A)
    return pl.pallas_call(
        functools.partial(_merge_kernel, d_model=d_model),
        grid=(t // tm,),
        in_specs=[grp(d) for d in DILATIONS_A] * 2 + [row(WIDTH_QB), row(WIDTH_C),
                  row(3 * d_model), row(d_model), const(wa), const(wb), const(wc), const(wo),
                  const(ln_g), const(ln_b)],
        out_specs=[row(d_model), pl.BlockSpec((tm * (d_model // SLAB_COLS), LANES), lambda i: (i, 0))],
        out_shape=[jax.ShapeDtypeStruct((t, d_model), F32),
                   jax.ShapeDtypeStruct((t * (d_model // SLAB_COLS), LANES), U32)],
        scratch_shapes=[pltpu.VMEM((n_grp, GROUP_WIDTH_A // LANES, tm, LANES), F32)] * 2,
        compiler_params=_cparams(("arbitrary",)),
        name="merge",
    )(*oas, *lses, yb, yc, gates, x2, wa, wb, wc, wo, ln_g, ln_b)


def _router_kernel(h_ref, whi_ref, wlo_ref, bias_ref, idx_ref, wgt_ref, cnt_ref, *, n_experts):
    h = h_ref[...]
    h_hi = h.astype(BF16)
    h_lo = (h - h_hi.astype(F32)).astype(BF16)
    logits = _dot_nt(whi_ref[...], h_hi) + _dot_nt(whi_ref[...], h_lo) + _dot_nt(wlo_ref[...], h_hi)
    scores = jax.nn.sigmoid(logits)
    biased = scores + bias_ref[...]
    tm = scores.shape[1]
    per_group = n_experts // N_EXPERT_GROUPS
    neg_inf = -jnp.inf

    b3 = biased.reshape(N_EXPERT_GROUPS, per_group, tm)
    i3 = lax.broadcasted_iota(jnp.int32, b3.shape, 1)
    m1 = jnp.max(b3, axis=1, keepdims=True)
    a1 = jnp.min(jnp.where(b3 == m1, i3, per_group), axis=1, keepdims=True)
    m2 = jnp.max(jnp.where(i3 == a1, neg_inf, b3), axis=1, keepdims=True)
    gscore = (m1 + m2).reshape(N_EXPERT_GROUPS, tm)

    gi = lax.broadcasted_iota(jnp.int32, gscore.shape, 0)
    chosen = jnp.zeros(gscore.shape, jnp.int32)
    for _ in range(TOPK_EXPERT_GROUPS):
        gm = jnp.max(gscore, axis=0, keepdims=True)
        ga = jnp.min(jnp.where(gscore == gm, gi, N_EXPERT_GROUPS), axis=0, keepdims=True)
        hit = gi == ga
        chosen = jnp.where(hit, 1, chosen)
        gscore = jnp.where(hit, neg_inf, gscore)

    cur = jnp.where(chosen.reshape(N_EXPERT_GROUPS, 1, tm) > 0, b3, neg_inf).reshape(n_experts, tm)
    ei = lax.broadcasted_iota(jnp.int32, cur.shape, 0)
    idxs, wgts = [], []
    member = jnp.zeros(cur.shape, F32)
    for _ in range(TOP_K):
        m = jnp.max(cur, axis=0, keepdims=True)
        a = jnp.min(jnp.where(cur == m, ei, n_experts), axis=0, keepdims=True)
        hit = ei == a
        idxs.append(a)
        wgts.append(jnp.sum(jnp.where(hit, scores, 0.0), axis=0, keepdims=True))
        member = jnp.where(hit, 1.0, member)
        cur = jnp.where(hit, neg_inf, cur)
    wsum = functools.reduce(jnp.add, wgts)
    idx_ref[...] = jnp.concatenate(idxs, axis=0)
    wgt_ref[...] = jnp.concatenate(wgts, axis=0) / wsum * ROUTED_SCALE

    @pl.when(pl.program_id(0) == 0)
    def _():
        cnt_ref[...] = jnp.zeros_like(cnt_ref)

    cnt_ref[...] += functools.reduce(
        jnp.add, [member[:, c * LANES:(c + 1) * LANES] for c in range(tm // LANES)])


def _router(h1, w_hi, w_lo, bias, tm):
    t, d_model = h1.shape
    n_experts = w_hi.shape[0]
    const = lambda a: pl.BlockSpec(a.shape, lambda i: (0, 0))
    return pl.pallas_call(
        functools.partial(_router_kernel, n_experts=n_experts),
        grid=(t // tm,),
        in_specs=[pl.BlockSpec((tm, d_model), lambda i: (i, 0)), const(w_hi), const(w_lo), const(bias)],
        out_specs=[pl.BlockSpec((TOP_K, tm), lambda i: (0, i)),
                   pl.BlockSpec((TOP_K, tm), lambda i: (0, i)),
                   pl.BlockSpec((n_experts, LANES), lambda i: (0, 0))],
        out_shape=[jax.ShapeDtypeStruct((TOP_K, t), jnp.int32), jax.ShapeDtypeStruct((TOP_K, t), F32),
                   jax.ShapeDtypeStruct((n_experts, LANES), F32)],
        compiler_params=_cparams(("arbitrary",)),
        name="router",
    )(h1, w_hi, w_lo, bias)


def _rank_kernel(idx_ref, base_ref, dest_ref, carry, *, n_experts):
    @pl.when(pl.program_id(0) == 0)
    def _():
        carry[...] = jnp.zeros_like(carry)

    idx = idx_ref[...]
    tm = idx.shape[1]
    ei = lax.broadcasted_iota(jnp.int32, (n_experts, tm), 0)
    hits = [ei == idx[k:k + 1, :] for k in range(TOP_K)]
    member = functools.reduce(jnp.add, [jnp.where(h, 1.0, 0.0) for h in hits])
    earlier = (lax.broadcasted_iota(jnp.int32, (tm, tm), 0)
               < lax.broadcasted_iota(jnp.int32, (tm, tm), 1))
    before = _dot(member.astype(BF16), jnp.where(earlier, 1.0, 0.0).astype(BF16))
    row = base_ref[...] + carry[...] + before
    dest = [jnp.sum(jnp.where(h, row, 0.0), axis=0, keepdims=True) for h in hits]
    dest_ref[...] = jnp.concatenate(dest, axis=0).astype(jnp.int32)
    carry[...] += jnp.sum(member, axis=1, keepdims=True)


def _ranks(top_idx, base, tm):
    k, t = top_idx.shape
    n_experts = base.shape[0]
    return pl.pallas_call(
        functools.partial(_rank_kernel, n_experts=n_experts),
        grid=(t // tm,),
        in_specs=[pl.BlockSpec((k, tm), lambda i: (0, i)), pl.BlockSpec((n_experts, 1), lambda i: (0, 0))],
        out_specs=pl.BlockSpec((k, tm), lambda i: (0, i)),
        out_shape=jax.ShapeDtypeStruct((k, t), jnp.int32),
        scratch_shapes=[pltpu.VMEM((n_experts, 1), F32)],
        compiler_params=_cparams(("arbitrary",)),
        name="ranks",
    )(top_idx, base)


def _expert_kernel(blk0_ref, nblk_ref, tab_hbm, h_hbm, wg_ref, wu_ref, wd_ref, out_hbm,
                   tab, wgu_s, wd_s, xb_s, xbuf, ybuf, gsem, ssem, tsem, *, n_tok, d_expert):
    n_real_rows = TOP_K * n_tok
    d_model = xb_s.shape[1]
    sl = d_model // SLAB_COLS
    e = pl.program_id(0)
    nb = nblk_ref[e]
    b0 = blk0_ref[e]

    def tab_copy(g):
        slot = (g + 1) & (TAB_RING - 1)
        return pltpu.make_async_copy(tab_hbm.at[pl.ds(g + 1, 1), :], tab.at[pl.ds(slot, 1), :], tsem.at[slot])

    def tab_start(g):
        tab_copy(g).start()

    def tab_wait(g):
        tab_copy(g).wait()

    def slab(ref, row):
        return ref.at[pl.ds(pl.multiple_of(row * sl, sl), sl), :]

    def start_gather(g):
        buf = g & (MOE_BUFS - 1)
        trow = (g + 1) & (TAB_RING - 1)
        for r in range(MOE_ROWS):
            tok = tab[trow, r] & (n_tok - 1)
            pltpu.make_async_copy(slab(h_hbm, tok), slab(xbuf.at[buf], r), gsem.at[buf]).start(priority=r % 2)

    def start_scatter(g):
        buf = g & (MOE_BUFS - 1)
        trow = (g + 1) & (TAB_RING - 1)
        for r in range(MOE_ROWS):
            pltpu.make_async_copy(slab(ybuf.at[buf], r), slab(out_hbm, tab[trow, r]),
                                  ssem.at[buf]).start(priority=r % 2)

    def wait_gather(g):
        buf = g & (MOE_BUFS - 1)
        pltpu.make_async_copy(h_hbm.at[pl.ds(0, MOE_ROWS * sl), :], xbuf.at[buf], gsem.at[buf]).wait()

    def wait_scatter(g):
        buf = g & (MOE_BUFS - 1)
        pltpu.make_async_copy(ybuf.at[buf], out_hbm.at[pl.ds(0, MOE_ROWS * sl), :], ssem.at[buf]).wait()

    @pl.when(e == 0)
    def _():
        for g in range(-1, MOE_AHEAD + 1):
            tab_start(g)
        ybuf[...] = jnp.zeros_like(ybuf)
        for g in range(-1, MOE_AHEAD):
            tab_wait(g)
        for buf in range(MOE_BUFS - 1):
            for r in range(MOE_ROWS):
                pltpu.make_async_copy(slab(ybuf.at[buf], r), slab(out_hbm, n_real_rows + buf * MOE_ROWS + r),
                                      ssem.at[buf]).start(priority=1)
        for g in range(MOE_AHEAD):
            start_gather(g)

    @pl.when(nb > 0)
    def _():
        wgu_s[:, :d_expert] = wg_ref[0].astype(BF16)
        wgu_s[:, d_expert:] = wu_ref[0].astype(BF16)
        wd_s[...] = wd_ref[0].astype(BF16)

        def block(b, _):
            g = b0 + b
            tab_start(g + MOE_AHEAD + 1)
            tab_wait(g + MOE_AHEAD)
            wait_gather(g)
            wait_scatter(g - MOE_BUFS)
            xb_s[...] = _slab_load(xbuf.at[g & (MOE_BUFS - 1)], MOE_ROWS, d_model).astype(BF16)
            start_gather(g + MOE_AHEAD)
            start_scatter(g - 1)
            gu = _dot(xb_s[...], wgu_s[...])
            act = gu[:, :d_expert]
            hid = (act * jax.nn.sigmoid(act) * gu[:, d_expert:]).astype(BF16)
            _slab_store(ybuf.at[g & (MOE_BUFS - 1)], _dot(hid, wd_s[...]))
            return 0

        lax.fori_loop(0, nb, block, 0)

    @pl.when(e == pl.num_programs(0) - 1)
    def _():
        n_act = b0 + nb
        start_scatter(n_act - 1)
        tab_wait(n_act + MOE_AHEAD)
        for i in range(MOE_BUFS):
            wait_scatter(n_act - 1 - i)
        for i in range(MOE_AHEAD):
            wait_gather(n_act + i)


def _slot_table_kernel(c0_ref, nv_ref, order_ref, tab_ref, *, n_real_rows):
    lane = lax.broadcasted_iota(jnp.int32, (1, LANES), 1)
    rows = []
    for j in range(tab_ref.shape[0]):
        gx = pl.program_id(0) * tab_ref.shape[0] + j
        c, n_valid = c0_ref[gx], nv_ref[gx]
        q, off = c >> 7, c & (LANES - 1)
        shift = (LANES - off) & (LANES - 1)
        a = pltpu.roll(order_ref[pl.ds(q, 1), :], shift, 1)
        b = pltpu.roll(order_ref[pl.ds(q + 1, 1), :], shift, 1)
        slots = jnp.where(lane < LANES - off, a, b)
        scratch = n_real_rows + ((gx - 1) & (MOE_BUFS - 1)) * MOE_ROWS + lane
        rows.append(jnp.where(lane < n_valid, slots, scratch))
    tab_ref[...] = jnp.concatenate(rows, axis=0)


def _slot_table(order, c0, nv, n_real_rows):
    n_tab = c0.shape[0]
    rows_per_step = 8
    return pl.pallas_call(
        functools.partial(_slot_table_kernel, n_real_rows=n_real_rows),
        grid_spec=pltpu.PrefetchScalarGridSpec(
            num_scalar_prefetch=2,
            grid=(n_tab // rows_per_step,),
            in_specs=[pl.BlockSpec(order.shape, lambda i, *_: (0, 0))],
            out_specs=pl.BlockSpec((rows_per_step, LANES), lambda i, *_: (i, 0)),
        ),
        out_shape=jax.ShapeDtypeStruct((n_tab, LANES), jnp.int32),
        compiler_params=_cparams(("arbitrary",)),
        name="slot_table",
    )(c0, nv, order)


def _experts(counts, dest, h1_slab, w_gate, w_up, w_down):
    n_experts, d_model, d_expert = w_gate.shape
    sl = d_model // SLAB_COLS
    n_tok = h1_slab.shape[0] // sl
    n_assign = TOP_K * n_tok
    assert n_tok & (n_tok - 1) == 0 and MOE_AHEAD + 3 <= TAB_RING and MOE_AHEAD < MOE_BUFS
    assert MOE_ROWS == LANES
    nblk = (counts + MOE_ROWS - 1) // MOE_ROWS
    blk_end = jnp.cumsum(nblk)
    blk0 = blk_end - nblk
    starts = jnp.cumsum(counts) - counts
    _, order = lax.sort_key_val(dest.reshape(n_assign), jnp.arange(n_assign, dtype=jnp.int32))
    order = jnp.concatenate([order, jnp.zeros((2 * LANES,), jnp.int32)]).reshape(n_assign // LANES + 2, LANES)
    n_tab = -(-(n_assign // MOE_ROWS + n_experts + MOE_AHEAD + 2) // 8) * 8
    g = jnp.arange(n_tab, dtype=jnp.int32) - 1
    ge = jnp.minimum(jnp.sum((blk_end[None, :] <= g[:, None]).astype(jnp.int32), axis=1), n_experts - 1)
    onehot = (ge[:, None] == jnp.arange(n_experts, dtype=jnp.int32)[None, :]).astype(jnp.int32)
    pick = lambda v: jnp.sum(onehot * v[None, :], axis=1)
    b = g - pick(blk0)
    live = (g >= 0) & (g < blk_end[-1])
    c0 = jnp.where(live, pick(starts) + b * MOE_ROWS, n_assign).astype(jnp.int32)
    nv = jnp.where(live, jnp.minimum(pick(counts) - b * MOE_ROWS, MOE_ROWS), 0).astype(jnp.int32)
    tab = _slot_table(order, c0, nv, n_assign)
    return pl.pallas_call(
        functools.partial(_expert_kernel, n_tok=n_tok, d_expert=d_expert),
        grid_spec=pltpu.PrefetchScalarGridSpec(
            num_scalar_prefetch=2,
            grid=(n_experts,),
            in_specs=[pl.BlockSpec(memory_space=pl.ANY),
                      pl.BlockSpec(memory_space=pl.ANY),
                      pl.BlockSpec((1, d_model, d_expert), lambda e, *_: (e, 0, 0)),
                      pl.BlockSpec((1, d_model, d_expert), lambda e, *_: (e, 0, 0)),
                      pl.BlockSpec((1, d_expert, d_model), lambda e, *_: (e, 0, 0))],
            out_specs=pl.BlockSpec(memory_space=pl.ANY),
            scratch_shapes=[pltpu.SMEM((TAB_RING, MOE_ROWS), jnp.int32),
                            pltpu.VMEM((d_model, 2 * d_expert), BF16),
                            pltpu.VMEM((d_expert, d_model), BF16),
                            pltpu.VMEM((MOE_ROWS, d_model), BF16),
                            pltpu.VMEM((MOE_BUFS, MOE_ROWS * sl, LANES), U32),
                            pltpu.VMEM((MOE_BUFS, MOE_ROWS * sl, LANES), U32),
                            pltpu.SemaphoreType.DMA((MOE_BUFS,)),
                            pltpu.SemaphoreType.DMA((MOE_BUFS,)),
                            pltpu.SemaphoreType.DMA((TAB_RING,))],
        ),
        out_shape=jax.ShapeDtypeStruct(((n_assign + MOE_BUFS * MOE_ROWS) * sl, LANES), U32),
        compiler_params=_cparams(("arbitrary",)),
        name="experts",
    )(blk0.astype(jnp.int32), nblk.astype(jnp.int32), tab, h1_slab, w_gate, w_up, w_down)


def _final_kernel(*refs, d_expert):
    h_ref, wt_ref = refs[0], refs[1]
    routed_refs = refs[2:2 + TOP_K]
    wgu_ref, wd_ref, g_ref, b_ref, o_ref = refs[2 + TOP_K:]
    h = h_ref[...]
    gu = _dot(h.astype(BF16), wgu_ref[...])
    act = gu[:, :d_expert]
    hid = (act * jax.nn.sigmoid(act) * gu[:, d_expert:]).astype(BF16)
    ff = _dot(hid, wd_ref[...])
    wt = wt_ref[...]
    rows, d_model = h.shape
    for k in range(TOP_K):
        ff += _slab_load(routed_refs[k], rows, d_model) * wt[:, k:k + 1]
    o_ref[...] = _layer_norm(ALPHA * h + ff, g_ref[...], b_ref[...])


def _final(h1, wt, routed, w_gu, w_d, ln_g, ln_b, tm):
    t, d_model = h1.shape
    d_expert = w_d.shape[0]
    per_k = t // tm
    row = lambda w: pl.BlockSpec((tm, w), lambda i: (i, 0))
    const = lambda a: pl.BlockSpec(a.shape, lambda i: (0, 0))
    routed_specs = [pl.BlockSpec((tm * (d_model // SLAB_COLS), LANES), lambda i, k=k: (k * per_k + i, 0))
                    for k in range(TOP_K)]
    return pl.pallas_call(
        functools.partial(_final_kernel, d_expert=d_expert),
        grid=(t // tm,),
        in_specs=[row(d_model), row(TOP_K)] + routed_specs + [const(w_gu), const(w_d), const(ln_g), const(ln_b)],
        out_specs=row(d_model),
        out_shape=jax.ShapeDtypeStruct((t, d_model), F32),
        compiler_params=_cparams(("arbitrary",)),
        name="final",
    )(h1, wt, *([routed] * TOP_K), w_gu, w_d, ln_g, ln_b)


def _layer(h, mem, w_in, w_mem_kv, sinks, w_a, w_b, w_c, w_out, ln1_g, ln1_b, router_w, router_bias,
           w_eg, w_eu, w_ed, w_sg, w_su, w_sd, ln2_g, ln2_b):
    bsz, seq_len, d_model = h.shape
    n_tok = bsz * seq_len
    n_mem = mem.shape[1]
    n_experts = router_w.shape[1]
    x2 = h.reshape(n_tok, d_model)
    xb = x2.astype(BF16)

    o_qa, o_ka, o_va = 0, WIDTH_A, 2 * WIDTH_A
    o_qb = 3 * WIDTH_A
    o_kb = o_qb + WIDTH_QB
    o_vb = o_kb + WIDTH_KVB
    o_qc = o_vb + WIDTH_KVB
    o_gate = o_qc + WIDTH_C
    wb16 = w_in.astype(BF16)
    cols = lambda o, n: wb16[:, o:o + n]
    chunks_a = WIDTH_A // LANES
    scale_a = HEAD_DIM_A ** -0.5
    tabs_a = _rope_tables(seq_len, HEAD_DIM_A)
    qkv_a = _proj(xb, cols(0, 3 * WIDTH_A),
                  ((("rope128", scale_a),) * chunks_a, (("rope128", 1.0),) * chunks_a,
                   (("plain", 1.0),) * chunks_a),
                  WIDTH_A, BF16, tabs_a, seq_len, dilations=DILATIONS_A)

    hb = HEAD_DIM_B
    dup = lambda o: [cols(o + n * hb, hb) for n in range(N_KV_HEADS_B) for _ in range(2)]
    w_qkv_b = jnp.concatenate([cols(o_qb, WIDTH_QB)] + dup(o_kb) + dup(o_vb), axis=1)
    q_chunks = WIDTH_QB // LANES
    modes_b = ((("rope64", HEAD_DIM_B ** -0.5),) * q_chunks + (("rope64", 1.0),) * N_KV_HEADS_B
               + (("plain", 1.0),) * N_KV_HEADS_B,)
    qkv_b = _proj(xb, w_qkv_b, modes_b, w_qkv_b.shape[1], BF16, _rope_tables(seq_len, HEAD_DIM_B), seq_len)

    q_c = _proj(xb, cols(o_qc, WIDTH_C), ((("plain", HEAD_DIM_C ** -0.5),) * (WIDTH_C // LANES),),
                WIDTH_C, BF16)
    gate_tn = 1536
    gates = _proj(xb, cols(o_gate, 3 * d_model),
                  ((("sigmoid", 1.0),) * (gate_tn // LANES),) * (3 * d_model // gate_tn), gate_tn, BF16)
    mkv = _proj(mem.reshape(bsz * n_mem, d_model).astype(BF16), w_mem_kv.astype(BF16),
                ((("plain", 1.0),) * (2 * WIDTH_C // LANES),), 2 * WIDTH_C, BF16)

    tq = min(512, seq_len // max(DILATIONS_A))
    oas, lses = [], []
    for d, qkv_g in zip(DILATIONS_A, qkv_a):
        first = (jnp.arange(n_tok // tq, dtype=jnp.int32) % (seq_len // d // tq) == 0).astype(jnp.int32)
        o_g, lse_g = _attn_a(qkv_g.reshape(n_tok, 3 * GROUP_WIDTH_A), first, tq)
        oas.append(o_g.reshape(bsz, d, seq_len // d, GROUP_WIDTH_A))
        lses.append(lse_g.reshape(bsz, d, seq_len // d, GROUP_WIDTH_A))

    yb = _attn_b(qkv_b, sinks.astype(F32), seq_len, tq)
    yc = _attn_c(q_c, mkv, seq_len, n_mem, tq)

    h1, h1_slab = _merge(oas, lses, yb, yc, gates, x2, w_a.astype(BF16), w_b.astype(BF16), w_c.astype(BF16),
                         w_out.astype(BF16), ln1_g.reshape(1, d_model), ln1_b.reshape(1, d_model), seq_len,
                         tm=256)

    rw_t = router_w.T
    rw_hi = rw_t.astype(BF16)
    rw_lo = (rw_t - rw_hi.astype(F32)).astype(BF16)
    top_idx, top_w, cnt = _router(h1, rw_hi, rw_lo, router_bias.reshape(n_experts, 1).astype(F32), tm=256)
    counts = jnp.sum(cnt, axis=1).astype(jnp.int32)
    first_row = (jnp.cumsum(counts) - counts).astype(F32).reshape(n_experts, 1)
    dest = _ranks(top_idx, first_row, tm=256)
    routed = _experts(counts, dest, h1_slab, w_eg, w_eu, w_ed)
    w_sgu = jnp.concatenate([w_sg, w_su], axis=1).astype(BF16)
    out = _final(h1, top_w.T, routed, w_sgu, w_sd.astype(BF16), ln2_g.reshape(1, d_model),
                 ln2_b.reshape(1, d_model), tm=256)
    return out.reshape(bsz, seq_len, d_model)


def kernel(x, mem, w_in, w_mem_kv, attn_sinks, w_branch_a, w_branch_b, w_branch_c, w_out, ln1_g, ln1_b,
           router_w, router_bias, w_exp_gate, w_exp_up, w_exp_down, w_sh_gate, w_sh_up, w_sh_down,
           ln2_g, ln2_b):
    h = x
    for layer in range(DEPTH):
        h = _layer(h, mem, w_in[layer], w_mem_kv[layer], attn_sinks[layer], w_branch_a[layer],
                   w_branch_b[layer], w_branch_c[layer], w_out[layer], ln1_g[layer], ln1_b[layer],
                   router_w[layer], router_bias[layer], w_exp_gate[layer], w_exp_up[layer],
                   w_exp_down[layer], w_sh_gate[layer], w_sh_up[layer], w_sh_down[layer],
                   ln2_g[layer], ln2_b[layer])
    return h
```
